```python
import jax, jax.numpy as jnp
from jax import lax
import numpy as np

D_MODEL = 1024
BATCH = 1
SEQ = 16384
DEPTH = 2
DEC_BATCH = 32
DEC_SEQ = 8
PAST_LEN = 16384
PAGE_SIZE = 128

HEAD_DIM = 64
D_A = 3 * D_MODEL // 8
N_HEADS_A = D_A // HEAD_DIM
DILATED_PATTERNS = ((128, 1), (512, 4), (2048, 16))
W_MAX = 2048
ROPE_THETA = 10000.0
D_B = D_MODEL // 4
POOL_WINDOWS = (2, 4, 8, 16)
N_POOL_GROUPS = 4
POOL_GROUP = D_B // N_POOL_GROUPS
POOL_MAX = 16
D_C = D_MODEL - D_A - D_B
C_BLOCK = 64
N_BLOCKS_C = D_C // C_BLOCK
CONV_W = 4
RG_C = 8.0
D_IN = 3 * D_A + D_B + 2 * D_C
D_FF = -(-8 * D_MODEL // (3 * 256)) * 256
EPS = 1e-6

kernel_name = "hybrid_dilated_pool_rglru_decoder_step"


def rms_norm(x, g):
    xf = x.astype(jnp.float32)
    y = xf * lax.rsqrt(jnp.mean(xf * xf, axis=-1, keepdims=True) + EPS)
    return (y * g.astype(jnp.float32)).astype(x.dtype)


def rope(x, pos):
    half = HEAD_DIM // 2
    inv = jnp.power(ROPE_THETA, -2.0 * jnp.arange(half, dtype=jnp.float32) / HEAD_DIM)
    ang = pos.astype(jnp.float32)[:, None] * inv[None, :]
    cos = jnp.cos(ang)[None, :, None, :]
    sin = jnp.sin(ang)[None, :, None, :]
    x1, x2 = x[..., :half], x[..., half:]
    return jnp.concatenate([x1 * cos - x2 * sin, x2 * cos + x1 * sin], axis=-1)


def dilated_band_prompt(q, k, v, window, dilation):
    n = window // dilation
    b, s, h, dh = q.shape
    span = n * dilation
    s_pad = -(-s // span) * span
    nb = s_pad // span
    m = s_pad // dilation

    def to_blocks(t):
        t = jnp.pad(t, ((0, 0), (0, s_pad - s), (0, 0), (0, 0))).reshape(b, m, dilation, h, dh)
        return t.transpose(0, 2, 1, 3, 4).reshape(b, dilation, nb, n, h, dh)

    def with_prev(t):
        prev = jnp.pad(t, ((0, 0), (0, 0), (1, 0), (0, 0), (0, 0), (0, 0)))[:, :, :-1]
        return jnp.concatenate([prev, t], axis=3)

    qb = to_blocks(q)
    kk = with_prev(to_blocks(k))
    vv = with_prev(to_blocks(v))
    scores = jnp.einsum("brnqhd,brnkhd->brnhqk", qb, kk) * (1.0 / np.sqrt(HEAD_DIM))
    qi = jnp.arange(n)[:, None]
    ki = jnp.arange(2 * n)[None, :]
    dist = n + qi - ki
    band = (dist >= 0) & (dist <= n)
    has_key = (jnp.arange(nb) > 0)[:, None, None] | (ki >= n)[None]
    mask = band[None] & has_key
    scores = jnp.where(mask[:, None], scores, -jnp.inf)
    lse = jax.nn.logsumexp(scores, axis=-1)
    p = jnp.exp(scores - lse[..., None])
    o = jnp.einsum("brnhqk,brnkhd->brnqhd", p, vv)
    o = o.reshape(b, dilation, m, h, dh).transpose(0, 2, 1, 3, 4).reshape(b, s_pad, h, dh)[:, :s]
    lse = lse.transpose(0, 1, 2, 4, 3).reshape(b, dilation, m, h).transpose(0, 2, 1, 3).reshape(b, s_pad, h)[:, :s]
    return o, lse


def dilated_gather_sample(q, k_all, v_all, window, dilation):
    n = window // dilation
    t = q.shape[1]
    L = k_all.shape[1] - t
    idx = L + jnp.arange(t)[:, None] - dilation * jnp.arange(n + 1)[None, :]
    valid = idx >= 0
    idx = jnp.maximum(idx, 0)
    kg = k_all[:, idx]
    vg = v_all[:, idx]
    scores = jnp.einsum("bthd,btjhd->bthj", q, kg) * (1.0 / np.sqrt(HEAD_DIM))
    scores = jnp.where(valid[None, :, None, :], scores, -jnp.inf)
    lse = jax.nn.logsumexp(scores, axis=-1)
    p = jnp.exp(scores - lse[..., None])
    return jnp.einsum("bthj,btjhd->bthd", p, vg), lse


def dilated_attention(q, k, v, pos, cache_k, cache_v):
    b, t = q.shape[:2]
    qf = rope(q.astype(jnp.float32), pos)
    kf = rope(k.astype(jnp.float32), pos)
    vf = v.astype(jnp.float32)
    outs, lses = [], []
    if cache_k is None:
        for window, dilation in DILATED_PATTERNS:
            o, l = dilated_band_prompt(qf, kf, vf, window, dilation)
            outs.append(o)
            lses.append(l)
        keep = min(W_MAX, t)
        new_k = kf[:, -keep:].astype(k.dtype)
        new_v = vf[:, -keep:].astype(v.dtype)
    else:
        k_all = jnp.concatenate([cache_k.astype(jnp.float32), kf], axis=1)
        v_all = jnp.concatenate([cache_v.astype(jnp.float32), vf], axis=1)
        for window, dilation in DILATED_PATTERNS:
            o, l = dilated_gather_sample(qf, k_all, v_all, window, dilation)
            outs.append(o)
            lses.append(l)
        keep = min(W_MAX, k_all.shape[1])
        new_k = k_all[:, -keep:].astype(cache_k.dtype)
        new_v = v_all[:, -keep:].astype(cache_v.dtype)
    alpha = jax.nn.softmax(jnp.stack(lses, axis=0), axis=0)
    o = jnp.einsum("gbth,gbthd->bthd", alpha, jnp.stack(outs, axis=0))
    return o.reshape(b, t, D_A), new_k, new_v


def pool_mixer(u, pos, buf, w_map, scale):
    b, t, _ = u.shape
    full = u if buf is None else jnp.concatenate([buf.astype(u.dtype), u], axis=1)
    lp = full.shape[1] - t
    cs = jnp.pad(jnp.cumsum(full.astype(jnp.float32), axis=1), ((0, 0), (1, 0), (0, 0)))
    hi_rows = lp + 1 + jnp.arange(t)
    hi = cs[:, lp + 1:]
    means = []
    for g, w in enumerate(POOL_WINDOWS):
        sl = slice(g * POOL_GROUP, (g + 1) * POOL_GROUP)
        lo = jnp.take(cs[..., sl], jnp.maximum(hi_rows - w, 0), axis=1)
        cnt = jnp.minimum(pos + 1, w).astype(jnp.float32)[None, :, None]
        means.append((hi[..., sl] - lo) / cnt)
    pooled = jnp.concatenate(means, axis=-1) - u.astype(jnp.float32)
    y = jnp.einsum("btgc,gcd->btgd", pooled.reshape(b, t, N_POOL_GROUPS, POOL_GROUP), w_map.astype(jnp.float32))
    y = y.reshape(b, t, D_B) * scale.astype(jnp.float32)
    return y, full[:, -(POOL_MAX - 1):]


def rglru_mixer(xr, gate, conv_buf, h0, conv_w, conv_b, gate_a_w, gate_a_b, gate_x_w, gate_x_b, lru_lambda):
    b, t, _ = xr.shape
    pre = jnp.zeros((b, CONV_W - 1, D_C), xr.dtype) if conv_buf is None else conv_buf.astype(xr.dtype)
    full = jnp.concatenate([pre, xr], axis=1)
    xc = conv_b.astype(jnp.float32) + sum(
        full[:, j:j + t].astype(jnp.float32) * conv_w[j].astype(jnp.float32) for j in range(CONV_W))
    xb = xc.reshape(b, t, N_BLOCKS_C, C_BLOCK)
    r = jax.nn.sigmoid(jnp.einsum("btnc,ncd->btnd", xb, gate_a_w.astype(jnp.float32)).reshape(b, t, D_C)
                       + gate_a_b.astype(jnp.float32))
    i = jax.nn.sigmoid(jnp.einsum("btnc,ncd->btnd", xb, gate_x_w.astype(jnp.float32)).reshape(b, t, D_C)
                       + gate_x_b.astype(jnp.float32))
    log_a = RG_C * r * jax.nn.log_sigmoid(lru_lambda.astype(jnp.float32))
    a = jnp.exp(log_a)
    inp = jnp.sqrt(-jnp.expm1(2.0 * log_a)) * i * xc
    if h0 is not None:
        inp = inp.at[:, 0].add(a[:, 0] * h0.astype(jnp.float32))

    def combine(e1, e2):
        a1, b1 = e1
        a2, b2 = e2
        return a1 * a2, a2 * b1 + b2

    _, h = lax.associative_scan(combine, (a, inp), axis=1)
    y = h * jax.nn.gelu(gate.astype(jnp.float32))
    new_h = h[:, -1] if h0 is None else h[:, -1].astype(h0.dtype)
    return y, full[:, -(CONV_W - 1):], new_h


def decoder_layer(x, pos, state, norm1_g, w_in, pool_w, pool_scale, conv_w, conv_b, gate_a_w, gate_a_b,
                  gate_x_w, gate_x_b, lru_lambda, w_out, norm2_g, w_gu, w_down):
    b, t, _ = x.shape
    if state is None:
        ck = cv = pbuf = cbuf = h0 = None
    else:
        ck, cv, pbuf, cbuf, h0 = state
    hn = rms_norm(x, norm1_g)
    proj = hn @ w_in
    cuts = [D_A, 2 * D_A, 3 * D_A, 3 * D_A + D_B, 3 * D_A + D_B + D_C]
    q, k, v, u, xr, gt = jnp.split(proj, cuts, axis=-1)
    hs = (b, t, N_HEADS_A, HEAD_DIM)
    o_a, nk, nv = dilated_attention(q.reshape(hs), k.reshape(hs), v.reshape(hs), pos, ck, cv)
    o_b, npool = pool_mixer(u, pos, pbuf, pool_w, pool_scale)
    o_c, nconv, nh = rglru_mixer(xr, gt, cbuf, h0, conv_w, conv_b, gate_a_w, gate_a_b, gate_x_w, gate_x_b,
                                 lru_lambda)
    mix = jnp.concatenate([o_a, o_b, o_c], axis=-1).astype(x.dtype) @ w_out
    x = x + mix
    h2 = rms_norm(x, norm2_g)
    g_, up = jnp.split(h2 @ w_gu, 2, axis=-1)
    x = x + (jax.nn.silu(g_) * up) @ w_down
    return x, (nk, nv, npool, nconv, nh)


def setup_inputs(seed: int = 0) -> dict:
    key = jax.random.key(seed)
    ks = jax.random.split(key, 24)
    f32 = jnp.float32
    w_buf = min(W_MAX, PAST_LEN)
    nrm = lambda k, shape, s=1.0: jax.random.normal(k, shape, f32) * s
    u_lam = jax.random.uniform(ks[15], (DEPTH, D_C), f32, 0.9, 0.999)
    return {
        "x_prompt": nrm(ks[0], (BATCH, SEQ, D_MODEL)),
        "x_sample": nrm(ks[1], (DEC_BATCH, DEC_SEQ, D_MODEL)),
        "cache_win_k": nrm(ks[2], (DEPTH, DEC_BATCH, w_buf, N_HEADS_A, HEAD_DIM)),
        "cache_win_v": nrm(ks[3], (DEPTH, DEC_BATCH, w_buf, N_HEADS_A, HEAD_DIM)),
        "state_pool": nrm(ks[4], (DEPTH, DEC_BATCH, POOL_MAX - 1, D_B)),
        "state_conv": nrm(ks[5], (DEPTH, DEC_BATCH, CONV_W - 1, D_C)),
        "state_rglru": nrm(ks[6], (DEPTH, DEC_BATCH, D_C), 0.5),
        "norm1_g": 1.0 + nrm(ks[7], (DEPTH, D_MODEL), 0.02),
        "w_in": nrm(ks[8], (DEPTH, D_MODEL, D_IN), D_MODEL ** -0.5),
        "pool_w": nrm(ks[9], (DEPTH, N_POOL_GROUPS, POOL_GROUP, POOL_GROUP), POOL_GROUP ** -0.5),
        "pool_scale": 1.0 + nrm(ks[10], (DEPTH, D_B), 0.1),
        "conv_w": nrm(ks[11], (DEPTH, CONV_W, D_C), CONV_W ** -0.5),
        "conv_b": nrm(ks[12], (DEPTH, D_C), 0.01),
        "gate_a_w": nrm(ks[13], (DEPTH, N_BLOCKS_C, C_BLOCK, C_BLOCK), C_BLOCK ** -0.5),
        "gate_a_b": nrm(ks[14], (DEPTH, D_C), 0.01),
        "gate_x_w": nrm(ks[16], (DEPTH, N_BLOCKS_C, C_BLOCK, C_BLOCK), C_BLOCK ** -0.5),
        "gate_x_b": nrm(ks[17], (DEPTH, D_C), 0.01),
        "lru_lambda": jnp.log(u_lam) - jnp.log1p(-u_lam),
        "w_out": nrm(ks[18], (DEPTH, D_MODEL, D_MODEL), D_MODEL ** -0.5),
        "norm2_g": 1.0 + nrm(ks[19], (DEPTH, D_MODEL), 0.02),
        "w_gu": nrm(ks[20], (DEPTH, D_MODEL, 2 * D_FF), D_MODEL ** -0.5),
        "w_down": nrm(ks[21], (DEPTH, D_FF, D_MODEL), D_FF ** -0.5),
        "final_g": 1.0 + nrm(ks[22], (D_MODEL,), 0.02),
    }


def reference(x_prompt, x_sample, cache_win_k, cache_win_v, state_pool, state_conv, state_rglru,
              norm1_g, w_in, pool_w, pool_scale, conv_w, conv_b, gate_a_w, gate_a_b, gate_x_w, gate_x_b,
              lru_lambda, w_out, norm2_g, w_gu, w_down, final_g):
    pos_p = jnp.arange(x_prompt.shape[1], dtype=jnp.int32)
    pos_s = PAST_LEN + jnp.arange(x_sample.shape[1], dtype=jnp.int32)
    hp, hs = x_prompt, x_sample
    st_p, st_s = [], []
    for l in range(DEPTH):
        lw = (norm1_g[l], w_in[l], pool_w[l], pool_scale[l], conv_w[l], conv_b[l], gate_a_w[l], gate_a_b[l],
              gate_x_w[l], gate_x_b[l], lru_lambda[l], w_out[l], norm2_g[l], w_gu[l], w_down[l])
        hp, sp = decoder_layer(hp, pos_p, None, *lw)
        hs, ss = decoder_layer(hs, pos_s, (cache_win_k[l], cache_win_v[l], state_pool[l], state_conv[l],
                                           state_rglru[l]), *lw)
        st_p.append(sp)
        st_s.append(ss)
    y_prompt = rms_norm(hp, final_g)
    y_sample = rms_norm(hs, final_g)
    p_win_k = jnp.stack([s[0] for s in st_p], axis=0)
    p_win_v = jnp.stack([s[1] for s in st_p], axis=0)
    p_pool = jnp.stack([s[2] for s in st_p], axis=0)
    p_conv = jnp.stack([s[3] for s in st_p], axis=0)
    p_rglru = jnp.stack([s[4] for s in st_p], axis=0)
    s_win_k = jnp.stack([s[0] for s in st_s], axis=0)
    s_win_v = jnp.stack([s[1] for s in st_s], axis=0)
    s_pool = jnp.stack([s[2] for s in st_s], axis=0)
    s_conv = jnp.stack([s[3] for s in st_s], axis=0)
    s_rglru = jnp.stack([s[4] for s in st_s], axis=0)
    return (y_prompt, y_sample, p_win_k, p_win_v, p_pool, p_conv, p_rglru,
            s_win_k, s_win_v, s_pool, s_conv, s_rglru)
```

```python
import functools

import numpy as np
import jax
import jax.numpy as jnp
from jax import lax
from jax.experimental import pallas as pl
from jax.experimental.pallas import tpu as pltpu

F32 = jnp.float32
BF16 = jnp.bfloat16

D_MODEL = 1024
HEAD_DIM = 64
D_A = 384
N_HEADS_A = D_A // HEAD_DIM
D_B = 256
D_C = 384
D_IN = 3 * D_A + D_B + 2 * D_C
D_FF = 2816
POOL_MAX = 16
CONV_W = 4
RG_C = 8.0
EPS = 1e-6
ROPE_THETA = 10000.0
W_MAX = 2048
PAST_LEN = 16384
DILATED_PATTERNS = ((128, 1), (512, 4), (2048, 16))
BAND = 128
MAX_DIL = 16
SUPER = BAND * MAX_DIL

LANES = 128
SUBLANES = 8
NEG = -1e30
VMEM_LIMIT = 56 * 1024 * 1024

_Q0, _K0, _V0, _U0, _X0, _G0 = 0, D_A, 2 * D_A, 3 * D_A, 3 * D_A + D_B, 3 * D_A + D_B + D_C


def _rms(x, g):
    return x * lax.rsqrt(jnp.mean(x * x, axis=-1, keepdims=True) + EPS) * g


def _rope_tile(t, cos, sin_signed):
    lane = lax.broadcasted_iota(jnp.int32, t.shape, 1)
    first = (lane % HEAD_DIM) < (HEAD_DIM // 2)
    swapped = jnp.where(first, pltpu.roll(t, LANES - HEAD_DIM // 2, axis=1), pltpu.roll(t, HEAD_DIM // 2, axis=1))
    return t * cos + swapped * sin_signed


def _log_sigmoid(x):
    return jnp.minimum(x, 0.0) - jnp.log1p(jnp.exp(-jnp.abs(x)))


def _project(x, g1, w_in):
    hn = _rms(x, g1)
    return jnp.dot(hn.astype(BF16), w_in, preferred_element_type=F32)


def _pool_out(shift0, shift1, u, cnt0, cnt1, wpool, pscale):
    lane = lax.broadcasted_iota(jnp.int32, cnt0.shape, 1)
    lo = lane < HEAD_DIM
    s = shift0(0) + shift0(1)
    s2 = s
    s = s + shift0(2) + shift0(3)
    mean0 = jnp.where(lo, s2, s) / cnt0
    s = shift1(0)
    for i in range(1, 8):
        s = s + shift1(i)
    s8 = s
    for i in range(8, 16):
        s = s + shift1(i)
    mean1 = jnp.where(lo, s8, s) / cnt1
    pooled = jnp.concatenate([mean0, mean1], axis=1) - u
    return jnp.dot(pooled.astype(BF16), wpool, preferred_element_type=F32) * pscale


def _rglru_terms(xshift, convw, convb, wg, bg, lam):
    xc = convb + (xshift(3) * convw[0:1] + xshift(2) * convw[1:2] + xshift(1) * convw[2:3] + xshift(0) * convw[3:4])
    g = jnp.dot(xc.astype(BF16), wg, preferred_element_type=F32) + bg
    r = jax.nn.sigmoid(g[:, :D_C])
    ig = jax.nn.sigmoid(g[:, D_C:])
    log_a = RG_C * r * _log_sigmoid(lam)
    a = jnp.exp(log_a)
    th = jnp.tanh(log_a)
    b = jnp.sqrt(-2.0 * th / (1.0 - th)) * ig * xc
    return a, b


def _scan_rows(a, b, limit, group):
    n = a.shape[0]
    row = lax.broadcasted_iota(jnp.int32, a.shape, 0) % group
    s = 1
    while s < limit:
        if s < SUBLANES or group < n:
            keep = row >= s
            a_sh = jnp.where(keep, pltpu.roll(a, s, axis=0), 1.0)
            b_sh = jnp.where(keep, pltpu.roll(b, s, axis=0), 0.0)
        else:
            a_sh = jnp.concatenate([jnp.ones((s, a.shape[1]), F32), a[: n - s]], axis=0)
            b_sh = jnp.concatenate([jnp.zeros((s, a.shape[1]), F32), b[: n - s]], axis=0)
        b = a * b_sh + b
        a = a * a_sh
        s *= 2
    return a, b


def _prompt_proj_kernel(x_ref, g1_ref, win_ref, cos_ref, sin_ref, wpool_ref, pscale_ref, convw_ref, convb_ref,
                        wg_ref, bg_ref, lam_ref,
                        q_ref, k_ref, v_ref, obc_ref, pst_ref, cst_ref, hst_ref,
                        ubuf, xbuf, hcar):
    i = pl.program_id(0)
    tm = x_ref.shape[0]

    @pl.when(i == 0)
    def _():
        ubuf[0:POOL_MAX, :] = jnp.zeros((POOL_MAX, D_B), F32)
        xbuf[0:SUBLANES, :] = jnp.zeros((SUBLANES, D_C), F32)
        hcar[...] = jnp.zeros((SUBLANES, D_C), F32)

    proj = _project(x_ref[...], g1_ref[...], win_ref[...])
    cos = cos_ref[...]
    sin = sin_ref[...]
    for hp in range(D_A // LANES):
        sl = slice(hp * LANES, (hp + 1) * LANES)
        q_ref[hp] = _rope_tile(proj[:, _Q0:_K0][:, sl], cos, sin) * (HEAD_DIM ** -0.5)
        k_ref[hp] = _rope_tile(proj[:, _K0:_V0][:, sl], cos, sin)
        v_ref[hp] = proj[:, _V0:_U0][:, sl]

    u = proj[:, _U0:_X0]
    xr = proj[:, _X0:_G0]
    gt = proj[:, _G0:]
    ubuf[POOL_MAX:POOL_MAX + tm, :] = u
    xbuf[SUBLANES:SUBLANES + tm, :] = xr

    pos1 = (i * tm + 1 + lax.broadcasted_iota(jnp.int32, (tm, LANES), 0)).astype(F32)
    lo = lax.broadcasted_iota(jnp.int32, (tm, LANES), 1) < HEAD_DIM
    cnt0 = jnp.minimum(pos1, jnp.where(lo, 2.0, 4.0))
    cnt1 = jnp.minimum(pos1, jnp.where(lo, 8.0, 16.0))
    o_b = _pool_out(lambda s: ubuf[pl.ds(POOL_MAX - s, tm), 0:LANES],
                    lambda s: ubuf[pl.ds(POOL_MAX - s, tm), LANES:D_B],
                    u, cnt0, cnt1, wpool_ref[...], pscale_ref[...])

    a, b = _rglru_terms(lambda s: xbuf[pl.ds(SUBLANES - s, tm), :], convw_ref[...], convb_ref[...],
                        wg_ref[...], bg_ref[...], lam_ref[...])
    a_cum, b_cum = _scan_rows(a, b, tm, tm)
    h = a_cum * hcar[SUBLANES - 1:SUBLANES, :] + b_cum
    o_c = h * jax.nn.gelu(gt)

    obc_ref[:, 0:D_B] = o_b
    obc_ref[:, D_B:] = o_c

    ubuf[0:POOL_MAX, :] = u[tm - POOL_MAX:, :]
    xbuf[0:SUBLANES, :] = xr[tm - SUBLANES:, :]
    hcar[...] = h[tm - SUBLANES:, :]
    pst_ref[...] = u[tm - POOL_MAX:, :]
    cst_ref[...] = xr[tm - SUBLANES:, :]
    hst_ref[...] = h[tm - SUBLANES:, :]


def _const_spec(shape):
    return pl.BlockSpec(shape, lambda *_: (0,) * len(shape))


def _prompt_proj(x, lw, cos, sin, tm):
    t = x.shape[0]
    n_hp = D_A // LANES
    qkv_shape = jax.ShapeDtypeStruct((n_hp, t, LANES), F32)
    qkv_spec = pl.BlockSpec((n_hp, tm, LANES), lambda i: (0, i, 0))
    return pl.pallas_call(
        _prompt_proj_kernel,
        grid=(t // tm,),
        in_specs=[
            pl.BlockSpec((tm, D_MODEL), lambda i: (i, 0)),
            _const_spec((1, D_MODEL)),
            _const_spec((D_MODEL, D_IN)),
            pl.BlockSpec((tm, LANES), lambda i: (i, 0)),
            pl.BlockSpec((tm, LANES), lambda i: (i, 0)),
            _const_spec((D_B, D_B)),
            _const_spec((1, D_B)),
            _const_spec((CONV_W, D_C)),
            _const_spec((1, D_C)),
            _const_spec((D_C, 2 * D_C)),
            _const_spec((1, 2 * D_C)),
            _const_spec((1, D_C)),
        ],
        out_specs=[
            qkv_spec, qkv_spec, qkv_spec,
            pl.BlockSpec((tm, D_B + D_C), lambda i: (i, 0)),
            _const_spec((POOL_MAX, D_B)),
            _const_spec((SUBLANES, D_C)),
            _const_spec((SUBLANES, D_C)),
        ],
        out_shape=[
            qkv_shape, qkv_shape, qkv_shape,
            jax.ShapeDtypeStruct((t, D_B + D_C), F32),
            jax.ShapeDtypeStruct((POOL_MAX, D_B), F32),
            jax.ShapeDtypeStruct((SUBLANES, D_C), F32),
            jax.ShapeDtypeStruct((SUBLANES, D_C), F32),
        ],
        scratch_shapes=[
            pltpu.VMEM((POOL_MAX + tm, D_B), F32),
            pltpu.VMEM((SUBLANES + tm, D_C), F32),
            pltpu.VMEM((SUBLANES, D_C), F32),
        ],
        compiler_params=pltpu.CompilerParams(dimension_semantics=("arbitrary",), vmem_limit_bytes=VMEM_LIMIT),
        name="prompt_proj",
    )(x, lw["g1"], lw["w_in"], cos, sin, lw["wpool"], lw["pscale"], lw["convw"], lw["convb"], lw["wg"], lw["bg"],
      lw["lam"])


def _band_bias(dil):
    slabs = MAX_DIL // dil
    rows = BAND // slabs
    row = lax.broadcasted_iota(jnp.int32, (BAND, 2 * BAND), 0)
    col = lax.broadcasted_iota(jnp.int32, (BAND, 2 * BAND), 1)
    qi = slabs * (row % rows) + row // rows
    ki = slabs * (col % (2 * rows)) + col // (2 * rows)
    valid = (ki >= qi) & (ki <= qi + BAND)
    bias = jnp.where(valid, 0.0, NEG)
    bias_first = jnp.where(valid & (ki >= BAND), 0.0, NEG)
    return bias, bias_first


def _prompt_attn_kernel(q_ref, kp_ref, kc_ref, vp_ref, vc_ref, o_ref, kcat, vcat, acc_s, m_s, l_s):
    i = pl.program_id(1)

    @pl.when(i == 0)
    def _():
        kcat[0:BAND, :] = jnp.zeros((BAND, MAX_DIL * LANES), F32)
        vcat[0:BAND, :] = jnp.zeros((BAND, MAX_DIL * LANES), F32)

    @pl.when(i > 0)
    def _():
        kcat[0:BAND, :] = kp_ref[...]
        vcat[0:BAND, :] = vp_ref[...]

    kcat[BAND:, :] = kc_ref[...]
    vcat[BAND:, :] = vc_ref[...]

    head0 = lax.broadcasted_iota(jnp.int32, (BAND, LANES), 1) < HEAD_DIM

    for stage, (_, dil) in enumerate(DILATED_PATTERNS):
        slabs = MAX_DIL // dil
        rows = BAND // slabs
        bias, bias_first = _band_bias(dil)
        bias_blk0 = jnp.where(i == 0, bias_first, bias)
        for blk in range(slabs):
            for res in range(dil):
                groups = [res + dil * c for c in range(slabs)]
                q_rows = slice(rows * blk, rows * (blk + 1))
                k_rows = slice(BAND + rows * (blk - 1), BAND + rows * (blk + 1))
                lanes = [slice(g * LANES, (g + 1) * LANES) for g in groups]
                qb = jnp.concatenate([q_ref[q_rows, ls] for ls in lanes], axis=0)
                kk = jnp.concatenate([kcat[k_rows, ls] for ls in lanes], axis=0).astype(BF16)
                vv = jnp.concatenate([vcat[k_rows, ls] for ls in lanes], axis=0).astype(BF16)
                b_u = bias_blk0 if blk == 0 else bias
                parts = []
                for hh in range(2):
                    qh = jnp.where(head0 if hh == 0 else jnp.logical_not(head0), qb, 0.0).astype(BF16)
                    s = lax.dot_general(qh, kk, (((1,), (1,)), ((), ())), preferred_element_type=F32) + b_u
                    m = jnp.max(s, axis=1, keepdims=True)
                    p = jnp.exp(s - m)
                    l = jnp.sum(p, axis=1, keepdims=True)
                    pv = jnp.dot(p.astype(BF16), vv, preferred_element_type=F32)
                    parts.append((m, l, pv))
                m_b = jnp.where(head0, parts[0][0], parts[1][0])
                l_b = jnp.where(head0, parts[0][1], parts[1][1])
                pv_b = jnp.where(head0, parts[0][2], parts[1][2])
                if stage > 0:
                    m_o = jnp.concatenate([m_s[q_rows, ls] for ls in lanes], axis=0)
                    l_o = jnp.concatenate([l_s[q_rows, ls] for ls in lanes], axis=0)
                    a_o = jnp.concatenate([acc_s[q_rows, ls] for ls in lanes], axis=0)
                    m_n = jnp.maximum(m_o, m_b)
                    w_o = jnp.exp(m_o - m_n)
                    w_b = jnp.exp(m_b - m_n)
                    l_b = l_o * w_o + l_b * w_b
                    pv_b = a_o * w_o + pv_b * w_b
                    m_b = m_n
                last = stage == len(DILATED_PATTERNS) - 1
                if last:
                    out = pv_b / l_b
                for c, ls in enumerate(lanes):
                    sub = slice(c * rows, (c + 1) * rows)
                    if last:
                        o_ref[q_rows, ls] = out[sub]
                    else:
                        m_s[q_rows, ls] = m_b[sub]
                        l_s[q_rows, ls] = l_b[sub]
                        acc_s[q_rows, ls] = pv_b[sub]


def _prompt_attn(q, k, v):
    n_hp, t, _ = q.shape
    n_super = t // SUPER
    width = MAX_DIL * LANES
    q16 = q.reshape(n_hp, t // MAX_DIL, width)
    k16 = k.reshape(n_hp, t // MAX_DIL, width)
    v16 = v.reshape(n_hp, t // MAX_DIL, width)
    cur = pl.BlockSpec((None, BAND, width), lambda h, i: (h, i, 0))
    prev = pl.BlockSpec((None, BAND, width), lambda h, i: (h, jnp.maximum(i - 1, 0), 0))
    o16 = pl.pallas_call(
        _prompt_attn_kernel,
        grid=(n_hp, n_super),
        in_specs=[cur, prev, cur, prev, cur],
        out_specs=cur,
        out_shape=jax.ShapeDtypeStruct(q16.shape, F32),
        scratch_shapes=[
            pltpu.VMEM((2 * BAND, width), F32),
            pltpu.VMEM((2 * BAND, width), F32),
            pltpu.VMEM((BAND, width), F32),
            pltpu.VMEM((BAND, width), F32),
            pltpu.VMEM((BAND, width), F32),
        ],
        compiler_params=pltpu.CompilerParams(dimension_semantics=("arbitrary", "arbitrary"),
                                             vmem_limit_bytes=VMEM_LIMIT),
        name="prompt_attn",
    )(q16, k16, k16, v16, v16)
    return o16.reshape(n_hp, t, LANES)


FF_CHUNK = 256


def _outffn_kernel(x_ref, oa_ref, obc_ref, wout_ref, g2_ref, wgu_ref, wdown_ref, fg_ref, o_ref, *, final):
    if len(oa_ref.shape) == 3:
        pieces = [oa_ref[hp] for hp in range(oa_ref.shape[0])]
    else:
        pieces = [oa_ref[...]]
    mix = jnp.concatenate(pieces + [obc_ref[...]], axis=1).astype(BF16)
    x1 = x_ref[...] + jnp.dot(mix, wout_ref[...], preferred_element_type=F32)
    h2 = _rms(x1, g2_ref[...]).astype(BF16)
    acc = x1
    for c in range(D_FF // FF_CHUNK):
        g = jnp.dot(h2, wgu_ref[:, c * FF_CHUNK:(c + 1) * FF_CHUNK], preferred_element_type=F32)
        up = jnp.dot(h2, wgu_ref[:, D_FF + c * FF_CHUNK:D_FF + (c + 1) * FF_CHUNK], preferred_element_type=F32)
        act = (jax.nn.silu(g) * up).astype(BF16)
        acc = acc + jnp.dot(act, wdown_ref[c * FF_CHUNK:(c + 1) * FF_CHUNK, :], preferred_element_type=F32)
    o_ref[...] = _rms(acc, fg_ref[...]) if final else acc


def _outffn(x, oa, obc, lw, final_g, final, tm):
    t = x.shape[0]
    if oa.ndim == 3:
        oa_spec = pl.BlockSpec((oa.shape[0], tm, LANES), lambda i: (0, i, 0))
    else:
        oa_spec = pl.BlockSpec((tm, D_A), lambda i: (i, 0))
    return pl.pallas_call(
        functools.partial(_outffn_kernel, final=final),
        grid=(t // tm,),
        in_specs=[
            pl.BlockSpec((tm, D_MODEL), lambda i: (i, 0)),
            oa_spec,
            pl.BlockSpec((tm, D_B + D_C), lambda i: (i, 0)),
            _const_spec((D_MODEL, D_MODEL)),
            _const_spec((1, D_MODEL)),
            _const_spec((D_MODEL, 2 * D_FF)),
            _const_spec((D_FF, D_MODEL)),
            _const_spec((1, D_MODEL)),
        ],
        out_specs=pl.BlockSpec((tm, D_MODEL), lambda i: (i, 0)),
        out_shape=jax.ShapeDtypeStruct((t, D_MODEL), F32),
        compiler_params=pltpu.CompilerParams(dimension_semantics=("parallel",), vmem_limit_bytes=VMEM_LIMIT),
        name="outffn",
    )(x, oa, obc, lw["w_out"], lw["g2"], lw["w_gu"], lw["w_down"], final_g)


def _sample_proj_kernel(x_ref, g1_ref, win_ref, cos_ref, sin_ref, wpool_ref, pscale_ref, convw_ref, convb_ref,
                        wg_ref, bg_ref, lam_ref, spool_ref, sconv_ref, h0_ref,
                        q_ref, k_ref, v_ref, obc_ref, npool_ref, nconv_ref, nh_ref,
                        ubuf, xbuf):
    n = x_ref.shape[0]
    nb = n // SUBLANES
    proj = _project(x_ref[...], g1_ref[...], win_ref[...])
    cos = cos_ref[...]
    sin = sin_ref[...]
    for hp in range(D_A // LANES):
        sl = slice(hp * LANES, (hp + 1) * LANES)
        q_ref[:, sl] = _rope_tile(proj[:, _Q0:_K0][:, sl], cos, sin) * (HEAD_DIM ** -0.5)
        k_ref[:, sl] = _rope_tile(proj[:, _K0:_V0][:, sl], cos, sin)
        v_ref[:, sl] = proj[:, _V0:_U0][:, sl]

    u = proj[:, _U0:_X0]
    xr = proj[:, _X0:_G0]
    gt = proj[:, _G0:]
    ubuf[:, 0:POOL_MAX, :] = spool_ref[...]
    ubuf[:, POOL_MAX:, :] = u.reshape(nb, SUBLANES, D_B)
    xbuf[:, 0:SUBLANES, :] = sconv_ref[...]
    xbuf[:, SUBLANES:, :] = xr.reshape(nb, SUBLANES, D_C)

    lo = lax.broadcasted_iota(jnp.int32, (n, LANES), 1) < HEAD_DIM
    cnt0 = jnp.where(lo, 2.0, 4.0)
    cnt1 = jnp.where(lo, 8.0, 16.0)
    o_b = _pool_out(lambda s: ubuf[:, pl.ds(POOL_MAX - s, SUBLANES), 0:LANES].reshape(n, LANES),
                    lambda s: ubuf[:, pl.ds(POOL_MAX - s, SUBLANES), LANES:D_B].reshape(n, LANES),
                    u, cnt0, cnt1, wpool_ref[...], pscale_ref[...])

    a, b = _rglru_terms(lambda s: xbuf[:, pl.ds(SUBLANES - s, SUBLANES), :].reshape(n, D_C), convw_ref[...],
                        convb_ref[...], wg_ref[...], bg_ref[...], lam_ref[...])
    a_cum, b_cum = _scan_rows(a, b, SUBLANES, SUBLANES)
    h = a_cum * h0_ref[...] + b_cum
    obc_ref[:, 0:D_B] = o_b
    obc_ref[:, D_B:] = h * jax.nn.gelu(gt)
    npool_ref[...] = ubuf[:, SUBLANES:, :]
    nconv_ref[...] = xr.reshape(nb, SUBLANES, D_C)
    nh_ref[...] = h


def _sample_proj(x, lw, cos, sin, spool, sconv, h0rows):
    n = x.shape[0]
    nb = n // SUBLANES
    shapes = [
        jax.ShapeDtypeStruct((n, D_A), F32), jax.ShapeDtypeStruct((n, D_A), F32), jax.ShapeDtypeStruct((n, D_A), F32),
        jax.ShapeDtypeStruct((n, D_B + D_C), F32),
        jax.ShapeDtypeStruct((nb, POOL_MAX, D_B), F32),
        jax.ShapeDtypeStruct((nb, SUBLANES, D_C), F32),
        jax.ShapeDtypeStruct((n, D_C), F32),
    ]
    args = (x, lw["g1"], lw["w_in"], cos, sin, lw["wpool"], lw["pscale"], lw["convw"], lw["convb"], lw["wg"],
            lw["bg"], lw["lam"], spool, sconv, h0rows)
    return pl.pallas_call(
        _sample_proj_kernel,
        grid=(1,),
        in_specs=[_const_spec(a.shape) for a in args],
        out_specs=[_const_spec(s.shape) for s in shapes],
        out_shape=shapes,
        scratch_shapes=[
            pltpu.VMEM((nb, POOL_MAX + SUBLANES, D_B), F32),
            pltpu.VMEM((nb, 2 * SUBLANES, D_C), F32),
        ],
        compiler_params=pltpu.CompilerParams(dimension_semantics=("arbitrary",), vmem_limit_bytes=VMEM_LIMIT),
        name="sample_proj",
    )(*args)


def _sample_key_counts(w_buf, t_new):
    k = np.arange(w_buf + LANES)[None, :]
    dist = w_buf + np.arange(t_new)[:, None] - k
    cnt = np.zeros(dist.shape, np.float32)
    for window, dil in DILATED_PATTERNS:
        cnt += ((dist >= 0) & (dist <= window) & (dist % dil == 0)).astype(np.float32)
    return np.tile(cnt, (N_HEADS_A, 1))


def _sample_attn_kernel(cnt_ref, ck_ref, cv_ref, q_ref, kn_ref, vn_ref, *rest):
    sk_ref, sv_ref, oa_ref = rest[-3:]
    w_buf = ck_ref.shape[0]
    t_new = q_ref.shape[0]
    ck = ck_ref[...]
    cv = cv_ref[...]
    kn = kn_ref[...]
    vn = vn_ref[...]
    sk_ref[0:w_buf - t_new, :] = ck[t_new:]
    sk_ref[w_buf - t_new:, :] = kn
    sv_ref[0:w_buf - t_new, :] = cv[t_new:]
    sv_ref[w_buf - t_new:, :] = vn

    pad = jnp.zeros((LANES - t_new, D_A), F32)
    kall = jnp.concatenate([ck, kn, pad], axis=0).astype(BF16)
    vall = jnp.concatenate([cv, vn, pad], axis=0).astype(BF16)
    qt = jnp.concatenate([q_ref[...]] * N_HEADS_A, axis=0)
    row = lax.broadcasted_iota(jnp.int32, qt.shape, 0)
    lane = lax.broadcasted_iota(jnp.int32, qt.shape, 1)
    own = (row // t_new) == (lane // HEAD_DIM)
    qbd = jnp.where(own, qt, 0.0).astype(BF16)
    s = lax.dot_general(qbd, kall, (((1,), (1,)), ((), ())), preferred_element_type=F32)
    cnt = cnt_ref[...]
    s = jnp.where(cnt > 0.0, s, NEG)
    m = jnp.max(s, axis=1, keepdims=True)
    p = jnp.exp(s - m) * cnt
    den = jnp.sum(p, axis=1, keepdims=True)
    o_n = jnp.dot(p.astype(BF16), vall, preferred_element_type=F32) / den
    out = jnp.zeros((t_new, D_A), F32)
    lane_o = lax.broadcasted_iota(jnp.int32, (t_new, D_A), 1) // HEAD_DIM
    for h in range(N_HEADS_A):
        out = jnp.where(lane_o == h, o_n[h * t_new:(h + 1) * t_new, :], out)
    oa_ref[...] = out


def _sample_attn(layer, cache_k, cache_v, q, kn, vn, win_bufs, counts):
    depth, nb, w_buf, _ = cache_k.shape
    t_new = q.shape[0] // nb
    cache_spec = pl.BlockSpec((None, None, w_buf, D_A), lambda b: (layer, b, 0, 0))
    tok_spec = pl.BlockSpec((t_new, D_A), lambda b: (b, 0))
    in_specs = [_const_spec(counts.shape), cache_spec, cache_spec, tok_spec, tok_spec, tok_spec]
    args = [counts, cache_k, cache_v, q, kn, vn]
    aliases = {}
    if win_bufs is not None:
        in_specs += [pl.BlockSpec(memory_space=pl.ANY)] * 2
        aliases = {len(args): 0, len(args) + 1: 1}
        args += list(win_bufs)
    win_shape = jax.ShapeDtypeStruct(cache_k.shape, F32)
    sk, sv, oa = pl.pallas_call(
        _sample_attn_kernel,
        grid=(nb,),
        in_specs=in_specs,
        out_specs=[cache_spec, cache_spec, tok_spec],
        out_shape=[win_shape, win_shape, jax.ShapeDtypeStruct(q.shape, F32)],
        input_output_aliases=aliases,
        compiler_params=pltpu.CompilerParams(dimension_semantics=("parallel",), vmem_limit_bytes=VMEM_LIMIT),
        name="sample_attn",
    )(*args)
    return (sk, sv), oa


def _rope_tables(pos):
    half = HEAD_DIM // 2
    inv = jnp.power(ROPE_THETA, -2.0 * jnp.arange(half, dtype=F32) / HEAD_DIM)
    ang = pos.astype(F32)[:, None] * inv[None, :]
    cos = jnp.tile(jnp.cos(ang), (1, LANES // half))
    sin = jnp.sin(ang)
    sin_signed = jnp.tile(jnp.concatenate([-sin, sin], axis=1), (1, LANES // HEAD_DIM))
    return cos, sin_signed


def _block_diag(w):
    n, c, d = w.shape
    eye = jnp.eye(n, dtype=w.dtype)
    return (eye[:, None, :, None] * w[:, :, None, :]).reshape(n * c, n * d)


def _layer_weights(l, norm1_g, w_in, pool_w, pool_scale, conv_w, conv_b, gate_a_w, gate_a_b, gate_x_w, gate_x_b,
                   lru_lambda, w_out, norm2_g, w_gu, w_down):
    return dict(
        g1=norm1_g[l][None, :],
        w_in=w_in[l].astype(BF16),
        wpool=_block_diag(pool_w[l]).astype(BF16),
        pscale=pool_scale[l][None, :],
        convw=conv_w[l],
        convb=conv_b[l][None, :],
        wg=jnp.concatenate([_block_diag(gate_a_w[l]), _block_diag(gate_x_w[l])], axis=1).astype(BF16),
        bg=jnp.concatenate([gate_a_b[l], gate_x_b[l]])[None, :],
        lam=lru_lambda[l][None, :],
        w_out=w_out[l].astype(BF16),
        g2=norm2_g[l][None, :],
        w_gu=w_gu[l].astype(BF16),
        w_down=w_down[l].astype(BF16),
    )


def kernel(x_prompt, x_sample, cache_win_k, cache_win_v, state_pool, state_conv, state_rglru, norm1_g, w_in, pool_w,
           pool_scale, conv_w, conv_b, gate_a_w, gate_a_b, gate_x_w, gate_x_b, lru_lambda, w_out, norm2_g, w_gu,
           w_down, final_g):
    batch, seq, _ = x_prompt.shape
    nb, t_new, _ = x_sample.shape
    depth, _, w_buf = cache_win_k.shape[:3]
    assert batch == 1 and seq % SUPER == 0 and t_new == SUBLANES and w_buf == W_MAX
    tm_proj = 256
    tm_ffn = 512

    cos_p, sin_p = _rope_tables(jnp.arange(seq, dtype=jnp.int32))
    cos_s, sin_s = _rope_tables(PAST_LEN + jnp.arange(t_new, dtype=jnp.int32))
    cos_s = jnp.tile(cos_s, (nb, 1))
    sin_s = jnp.tile(sin_s, (nb, 1))
    counts = jnp.asarray(_sample_key_counts(w_buf, t_new))
    fg = final_g[None, :]

    ck = cache_win_k.reshape(depth, nb, w_buf, D_A)
    cv = cache_win_v.reshape(depth, nb, w_buf, D_A)
    spool = jnp.pad(state_pool, ((0, 0), (0, 0), (1, 0), (0, 0)))
    sconv = jnp.pad(state_conv, ((0, 0), (0, 0), (SUBLANES - (CONV_W - 1), 0), (0, 0)))
    h0rows = jnp.repeat(state_rglru, t_new, axis=1)

    hp = x_prompt.reshape(seq, D_MODEL)
    hs = x_sample.reshape(nb * t_new, D_MODEL)
    win_bufs = None
    p_k, p_v, p_pool, p_conv, p_h, s_pool, s_conv, s_h = [], [], [], [], [], [], [], []
    keep = min(W_MAX, seq)
    for l in range(depth):
        lw = _layer_weights(l, norm1_g, w_in, pool_w, pool_scale, conv_w, conv_b, gate_a_w, gate_a_b, gate_x_w,
                            gate_x_b, lru_lambda, w_out, norm2_g, w_gu, w_down)
        last = l == depth - 1
        q, k, v, obc, pst, cst, hst = _prompt_proj(hp, lw, cos_p, sin_p, tm_proj)
        oa = _prompt_attn(q, k, v)
        hp = _outffn(hp, oa, obc, lw, fg, last, tm_ffn)
        p_k.append(jnp.transpose(k[:, seq - keep:, :], (1, 0, 2)).reshape(1, keep, N_HEADS_A, HEAD_DIM))
        p_v.append(jnp.transpose(v[:, seq - keep:, :], (1, 0, 2)).reshape(1, keep, N_HEADS_A, HEAD_DIM))
        p_pool.append(pst[None, 1:, :])
        p_conv.append(cst[None, SUBLANES - (CONV_W - 1):, :])
        p_h.append(hst[SUBLANES - 1:, :])
        qs, kn, vn, obc_s, npool, nconv, nh = _sample_proj(hs, lw, cos_s, sin_s, spool[l], sconv[l], h0rows[l])
        win_bufs, oa_s = _sample_attn(l, ck, cv, qs, kn, vn, win_bufs, counts)
        hs = _outffn(hs, oa_s, obc_s, lw, fg, last, nb * t_new)
        s_pool.append(npool[:, 1:, :])
        s_conv.append(nconv[:, SUBLANES - (CONV_W - 1):, :])
        s_h.append(nh.reshape(nb, t_new, D_C)[:, t_new - 1, :])

    y_prompt = hp.reshape(batch, seq, D_MODEL)
    y_sample = hs.reshape(nb, t_new, D_MODEL)
    s_win_k = win_bufs[0].reshape(depth, nb, w_buf, N_HEADS_A, HEAD_DIM)
    s_win_v = win_bufs[1].reshape(depth, nb, w_buf, N_HEADS_A, HEAD_DIM)
    return (y_prompt, y_sample, jnp.stack(p_k), jnp.stack(p_v), jnp.stack(p_pool), jnp.stack(p_conv),
            jnp.stack(p_h), s_win_k, s_win_v, jnp.stack(s_pool), jnp.stack(s_conv), jnp.stack(s_h))
```

```python
import functools

import numpy as np
import jax
import jax.numpy as jnp
from jax import lax
from jax.experimental import pallas as pl
from jax.experimental.pallas import tpu as pltpu

F32 = jnp.float32
BF16 = jnp.bfloat16

D_MODEL = 1024
HEAD_DIM = 64
D_A = 384
N_HEADS_A = D_A // HEAD_DIM
D_B = 256
D_C = 384
D_IN = 3 * D_A + D_B + 2 * D_C
D_FF = 2816
POOL_MAX = 16
CONV_W = 4
RG_C = 8.0
EPS = 1e-6
ROPE_THETA = 10000.0
W_MAX = 2048
PAST_LEN = 16384
DILATED_PATTERNS = ((128, 1), (512, 4), (2048, 16))
BAND = 128
MAX_DIL = 16
SUPER = BAND * MAX_DIL

LANES = 128
SUBLANES = 8
NEG = -1e30
VMEM_LIMIT = 56 * 1024 * 1024

_Q0, _K0, _V0, _U0, _X0, _G0 = 0, D_A, 2 * D_A, 3 * D_A, 3 * D_A + D_B, 3 * D_A + D_B + D_C


def _rms(x, g):
    return x * lax.rsqrt(jnp.mean(x * x, axis=-1, keepdims=True) + EPS) * g


def _rope_tile(t, cos, sin_signed):
    lane = lax.broadcasted_iota(jnp.int32, t.shape, 1)
    first = (lane % HEAD_DIM) < (HEAD_DIM // 2)
    swapped = jnp.where(first, pltpu.roll(t, LANES - HEAD_DIM // 2, axis=1), pltpu.roll(t, HEAD_DIM // 2, axis=1))
    return t * cos + swapped * sin_signed


def _log_sigmoid(x):
    return jnp.minimum(x, 0.0) - jnp.log1p(jnp.exp(-jnp.abs(x)))


def _project(x, g1, w_in):
    hn = _rms(x, g1)
    return jnp.dot(hn.astype(BF16), w_in, preferred_element_type=F32)


def _pool_out(shift0, shift1, u, cnt0, cnt1, wpool, pscale):
    lane = lax.broadcasted_iota(jnp.int32, cnt0.shape, 1)
    lo = lane < HEAD_DIM
    s = shift0(0) + shift0(1)
    s2 = s
    s = s + shift0(2) + shift0(3)
    mean0 = jnp.where(lo, s2, s) / cnt0
    s = shift1(0)
    for i in range(1, 8):
        s = s + shift1(i)
    s8 = s
    for i in range(8, 16):
        s = s + shift1(i)
    mean1 = jnp.where(lo, s8, s) / cnt1
    pooled = jnp.concatenate([mean0, mean1], axis=1) - u
    return jnp.dot(pooled.astype(BF16), wpool, preferred_element_type=F32) * pscale


def _rglru_terms(xshift, convw, convb, wg, bg, lam):
    xc = convb + (xshift(3) * convw[0:1] + xshift(2) * convw[1:2] + xshift(1) * convw[2:3] + xshift(0) * convw[3:4])
    g = jnp.dot(xc.astype(BF16), wg, preferred_element_type=F32) + bg
    r = jax.nn.sigmoid(g[:, :D_C])
    ig = jax.nn.sigmoid(g[:, D_C:])
    log_a = RG_C * r * _log_sigmoid(lam)
    a = jnp.exp(log_a)
    th = jnp.tanh(log_a)
    b = jnp.sqrt(-2.0 * th / (1.0 - th)) * ig * xc
    return a, b


def _scan_rows(a, b, limit, group):
    n = a.shape[0]
    row = lax.broadcasted_iota(jnp.int32, a.shape, 0) % group
    s = 1
    while s < limit:
        if s < SUBLANES or group < n:
            keep = row >= s
            a_sh = jnp.where(keep, pltpu.roll(a, s, axis=0), 1.0)
            b_sh = jnp.where(keep, pltpu.roll(b, s, axis=0), 0.0)
        else:
            a_sh = jnp.concatenate([jnp.ones((s, a.shape[1]), F32), a[: n - s]], axis=0)
            b_sh = jnp.concatenate([jnp.zeros((s, a.shape[1]), F32), b[: n - s]], axis=0)
        b = a * b_sh + b
        a = a * a_sh
        s *= 2
    return a, b


def _prompt_proj_kernel(x_ref, g1_ref, win_ref, cos_ref, sin_ref, wpool_ref, pscale_ref, convw_ref, convb_ref,
                        wg_ref, bg_ref, lam_ref,
                        q_ref, k_ref, v_ref, obc_ref, pst_ref, cst_ref, hst_ref,
                        ubuf, xbuf, hcar):
    i = pl.program_id(0)
    tm = x_ref.shape[0]

    @pl.when(i == 0)
    def _():
        ubuf[0:POOL_MAX, :] = jnp.zeros((POOL_MAX, D_B), F32)
        xbuf[0:SUBLANES, :] = jnp.zeros((SUBLANES, D_C), F32)
        hcar[...] = jnp.zeros((SUBLANES, D_C), F32)

    proj = _project(x_ref[...], g1_ref[...], win_ref[...])
    cos = cos_ref[...]
    sin = sin_ref[...]
    for hp in range(D_A // LANES):
        sl = slice(hp * LANES, (hp + 1) * LANES)
        q_ref[hp] = _rope_tile(proj[:, _Q0:_K0][:, sl], cos, sin) * (HEAD_DIM ** -0.5)
        k_ref[hp] = _rope_tile(proj[:, _K0:_V0][:, sl], cos, sin)
        v_ref[hp] = proj[:, _V0:_U0][:, sl]

    u = proj[:, _U0:_X0]
    xr = proj[:, _X0:_G0]
    gt = proj[:, _G0:]
    ubuf[POOL_MAX:POOL_MAX + tm, :] = u
    xbuf[SUBLANES:SUBLANES + tm, :] = xr

    pos1 = (i * tm + 1 + lax.broadcasted_iota(jnp.int32, (tm, LANES), 0)).astype(F32)
    lo = lax.broadcasted_iota(jnp.int32, (tm, LANES), 1) < HEAD_DIM
    cnt0 = jnp.minimum(pos1, jnp.where(lo, 2.0, 4.0))
    cnt1 = jnp.minimum(pos1, jnp.where(lo, 8.0, 16.0))
    o_b = _pool_out(lambda s: ubuf[pl.ds(POOL_MAX - s, tm), 0:LANES],
                    lambda s: ubuf[pl.ds(POOL_MAX - s, tm), LANES:D_B],
                    u, cnt0, cnt1, wpool_ref[...], pscale_ref[...])

    a, b = _rglru_terms(lambda s: xbuf[pl.ds(SUBLANES - s, tm), :], convw_ref[...], convb_ref[...],
                        wg_ref[...], bg_ref[...], lam_ref[...])
    a_cum, b_cum = _scan_rows(a, b, tm, tm)
    h = a_cum * hcar[SUBLANES - 1:SUBLANES, :] + b_cum
    o_c = h * jax.nn.gelu(gt)

    obc_ref[:, 0:D_B] = o_b
    obc_ref[:, D_B:] = o_c

    ubuf[0:POOL_MAX, :] = u[tm - POOL_MAX:, :]
    xbuf[0:SUBLANES, :] = xr[tm - SUBLANES:, :]
    hcar[...] = h[tm - SUBLANES:, :]
    pst_ref[...] = u[tm - POOL_MAX:, :]
    cst_ref[...] = xr[tm - SUBLANES:, :]
    hst_ref[...] = h[tm - SUBLANES:, :]


def _const_spec(shape):
    return pl.BlockSpec(shape, lambda *_: (0,) * len(shape))


def _prompt_proj(x, lw, cos, sin, tm):
    t = x.shape[0]
    n_hp = D_A // LANES
    qkv_shape = jax.ShapeDtypeStruct((n_hp, t, LANES), F32)
    qkv_spec = pl.BlockSpec((n_hp, tm, LANES), lambda i: (0, i, 0))
    return pl.pallas_call(
        _prompt_proj_kernel,
        grid=(t // tm,),
        in_specs=[
            pl.BlockSpec((tm, D_MODEL), lambda i: (i, 0)),
            _const_spec((1, D_MODEL)),
            _const_spec((D_MODEL, D_IN)),
            pl.BlockSpec((tm, LANES), lambda i: (i, 0)),
            pl.BlockSpec((tm, LANES), lambda i: (i, 0)),
            _const_spec((D_B, D_B)),
            _const_spec((1, D_B)),
            _const_spec((CONV_W, D_C)),
            _const_spec((1, D_C)),
            _const_spec((D_C, 2 * D_C)),
            _const_spec((1, 2 * D_C)),
            _const_spec((1, D_C)),
        ],
        out_specs=[
            qkv_spec, qkv_spec, qkv_spec,
            pl.BlockSpec((tm, D_B + D_C), lambda i: (i, 0)),
            _const_spec((POOL_MAX, D_B)),
            _const_spec((SUBLANES, D_C)),
            _const_spec((SUBLANES, D_C)),
        ],
        out_shape=[
            qkv_shape, qkv_shape, qkv_shape,
            jax.ShapeDtypeStruct((t, D_B + D_C), F32),
            jax.ShapeDtypeStruct((POOL_MAX, D_B), F32),
            jax.ShapeDtypeStruct((SUBLANES, D_C), F32),
            jax.ShapeDtypeStruct((SUBLANES, D_C), F32),
        ],
        scratch_shapes=[
            pltpu.VMEM((POOL_MAX + tm, D_B), F32),
            pltpu.VMEM((SUBLANES + tm, D_C), F32),
            pltpu.VMEM((SUBLANES, D_C), F32),
        ],
        compiler_params=pltpu.CompilerParams(dimension_semantics=("arbitrary",), vmem_limit_bytes=VMEM_LIMIT),
        name="prompt_proj",
    )(x, lw["g1"], lw["w_in"], cos, sin, lw["wpool"], lw["pscale"], lw["convw"], lw["convb"], lw["wg"], lw["bg"],
      lw["lam"])


def _band_bias(dil):
    slabs = MAX_DIL // dil
    rows = BAND // slabs
    row = lax.broadcasted_iota(jnp.int32, (BAND, 2 * BAND), 0)
    col = lax.broadcasted_iota(jnp.int32, (BAND, 2 * BAND), 1)
    qi = slabs * (row % rows) + row // rows
    ki = slabs * (col % (2 * rows)) + col // (2 * rows)
    valid = (ki >= qi) & (ki <= qi + BAND)
    bias = jnp.where(valid, 0.0, NEG)
    bias_first = jnp.where(valid & (ki >= BAND), 0.0, NEG)
    return bias, bias_first


def _prompt_attn_kernel(q_ref, k_ref, v_ref, o_ref, q16, kcat, vcat, acc_s, m_s, l_s):
    i = pl.program_id(1)

    @pl.when(i == 0)
    def _():
        kcat[0:BAND, :] = jnp.zeros((BAND, MAX_DIL * LANES), F32)
        vcat[0:BAND, :] = jnp.zeros((BAND, MAX_DIL * LANES), F32)

    @pl.when(i > 0)
    def _():
        kcat[0:BAND, :] = kcat[BAND:, :]
        vcat[0:BAND, :] = vcat[BAND:, :]

    for r in range(MAX_DIL):
        ls = slice(r * LANES, (r + 1) * LANES)
        q16[:, ls] = q_ref[pl.ds(r, BAND, stride=MAX_DIL), :]
        kcat[BAND:, ls] = k_ref[pl.ds(r, BAND, stride=MAX_DIL), :]
        vcat[BAND:, ls] = v_ref[pl.ds(r, BAND, stride=MAX_DIL), :]

    head0 = lax.broadcasted_iota(jnp.int32, (BAND, LANES), 1) < HEAD_DIM

    for stage, (_, dil) in enumerate(DILATED_PATTERNS):
        slabs = MAX_DIL // dil
        rows = BAND // slabs
        bias, bias_first = _band_bias(dil)
        bias_blk0 = jnp.where(i == 0, bias_first, bias)
        for blk in range(slabs):
            for res in range(dil):
                groups = [res + dil * c for c in range(slabs)]
                q_rows = slice(rows * blk, rows * (blk + 1))
                k_rows = slice(BAND + rows * (blk - 1), BAND + rows * (blk + 1))
                lanes = [slice(g * LANES, (g + 1) * LANES) for g in groups]
                qb = jnp.concatenate([q16[q_rows, ls] for ls in lanes], axis=0)
                kk = jnp.concatenate([kcat[k_rows, ls] for ls in lanes], axis=0).astype(BF16)
                vv = jnp.concatenate([vcat[k_rows, ls] for ls in lanes], axis=0).astype(BF16)
                b_u = bias_blk0 if blk == 0 else bias
                parts = []
                for hh in range(2):
                    qh = jnp.where(head0 if hh == 0 else jnp.logical_not(head0), qb, 0.0).astype(BF16)
                    s = lax.dot_general(qh, kk, (((1,), (1,)), ((), ())), preferred_element_type=F32) + b_u
                    m = jnp.max(s, axis=1, keepdims=True)
                    p = jnp.exp(s - m)
                    l = jnp.sum(p, axis=1, keepdims=True)
                    pv = jnp.dot(p.astype(BF16), vv, preferred_element_type=F32)
                    parts.append((m, l, pv))
                m_b = jnp.where(head0, parts[0][0], parts[1][0])
                l_b = jnp.where(head0, parts[0][1], parts[1][1])
                pv_b = jnp.where(head0, parts[0][2], parts[1][2])
                if stage > 0:
                    m_o = jnp.concatenate([m_s[q_rows, ls] for ls in lanes], axis=0)
                    l_o = jnp.concatenate([l_s[q_rows, ls] for ls in lanes], axis=0)
                    a_o = jnp.concatenate([acc_s[q_rows, ls] for ls in lanes], axis=0)
                    m_n = jnp.maximum(m_o, m_b)
                    w_o = jnp.exp(m_o - m_n)
                    w_b = jnp.exp(m_b - m_n)
                    l_b = l_o * w_o + l_b * w_b
                    pv_b = a_o * w_o + pv_b * w_b
                    m_b = m_n
                last = stage == len(DILATED_PATTERNS) - 1
                if last:
                    out = pv_b / l_b
                for c, ls in enumerate(lanes):
                    sub = slice(c * rows, (c + 1) * rows)
                    if last:
                        o_ref[pl.ds(MAX_DIL * rows * blk + groups[c], rows, stride=MAX_DIL), :] = out[sub]
                    else:
                        m_s[q_rows, ls] = m_b[sub]
                        l_s[q_rows, ls] = l_b[sub]
                        acc_s[q_rows, ls] = pv_b[sub]


def _prompt_attn(q, k, v):
    n_hp, t, _ = q.shape
    n_super = t // SUPER
    width = MAX_DIL * LANES
    cur = pl.BlockSpec((None, SUPER, LANES), lambda h, i: (h, i, 0))
    return pl.pallas_call(
        _prompt_attn_kernel,
        grid=(n_hp, n_super),
        in_specs=[cur, cur, cur],
        out_specs=cur,
        out_shape=jax.ShapeDtypeStruct(q.shape, F32),
        scratch_shapes=[
            pltpu.VMEM((BAND, width), F32),
            pltpu.VMEM((2 * BAND, width), F32),
            pltpu.VMEM((2 * BAND, width), F32),
            pltpu.VMEM((BAND, width), F32),
            pltpu.VMEM((BAND, width), F32),
            pltpu.VMEM((BAND, width), F32),
        ],
        compiler_params=pltpu.CompilerParams(dimension_semantics=("arbitrary", "arbitrary"),
                                             vmem_limit_bytes=VMEM_LIMIT),
        name="prompt_attn",
    )(q, k, v)


FF_CHUNK = 256


def _outffn_kernel(x_ref, oa_ref, obc_ref, wout_ref, g2_ref, wgu_ref, wdown_ref, fg_ref, o_ref, *, final):
    if len(oa_ref.shape) == 3:
        pieces = [oa_ref[hp] for hp in range(oa_ref.shape[0])]
    else:
        pieces = [oa_ref[...]]
    mix = jnp.concatenate(pieces + [obc_ref[...]], axis=1).astype(BF16)
    x1 = x_ref[...] + jnp.dot(mix, wout_ref[...], preferred_element_type=F32)
    h2 = _rms(x1, g2_ref[...]).astype(BF16)
    acc = x1
    for c in range(D_FF // FF_CHUNK):
        g = jnp.dot(h2, wgu_ref[:, c * FF_CHUNK:(c + 1) * FF_CHUNK], preferred_element_type=F32)
        up = jnp.dot(h2, wgu_ref[:, D_FF + c * FF_CHUNK:D_FF + (c + 1) * FF_CHUNK], preferred_element_type=F32)
        act = (jax.nn.silu(g) * up).astype(BF16)
        acc = acc + jnp.dot(act, wdown_ref[c * FF_CHUNK:(c + 1) * FF_CHUNK, :], preferred_element_type=F32)
    o_ref[...] = _rms(acc, fg_ref[...]) if final else acc


def _outffn(x, oa, obc, lw, final_g, final, tm):
    t = x.shape[0]
    if oa.ndim == 3:
        oa_spec = pl.BlockSpec((oa.shape[0], tm, LANES), lambda i: (0, i, 0))
    else:
        oa_spec = pl.BlockSpec((tm, D_A), lambda i: (i, 0))
    return pl.pallas_call(
        functools.partial(_outffn_kernel, final=final),
        grid=(t // tm,),
        in_specs=[
            pl.BlockSpec((tm, D_MODEL), lambda i: (i, 0)),
            oa_spec,
            pl.BlockSpec((tm, D_B + D_C), lambda i: (i, 0)),
            _const_spec((D_MODEL, D_MODEL)),
            _const_spec((1, D_MODEL)),
            _const_spec((D_MODEL, 2 * D_FF)),
            _const_spec((D_FF, D_MODEL)),
            _const_spec((1, D_MODEL)),
        ],
        out_specs=pl.BlockSpec((tm, D_MODEL), lambda i: (i, 0)),
        out_shape=jax.ShapeDtypeStruct((t, D_MODEL), F32),
        compiler_params=pltpu.CompilerParams(dimension_semantics=("parallel",), vmem_limit_bytes=VMEM_LIMIT),
        name="outffn",
    )(x, oa, obc, lw["w_out"], lw["g2"], lw["w_gu"], lw["w_down"], final_g)


def _sample_proj_kernel(x_ref, g1_ref, win_ref, cos_ref, sin_ref, wpool_ref, pscale_ref, convw_ref, convb_ref,
                        wg_ref, bg_ref, lam_ref, spool_ref, sconv_ref, h0_ref,
                        q_ref, k_ref, v_ref, obc_ref, npool_ref, nconv_ref, nh_ref,
                        ubuf, xbuf):
    n = x_ref.shape[0]
    nb = n // SUBLANES
    proj = _project(x_ref[...], g1_ref[...], win_ref[...])
    cos = cos_ref[...]
    sin = sin_ref[...]
    for hp in range(D_A // LANES):
        sl = slice(hp * LANES, (hp + 1) * LANES)
        q_ref[:, sl] = _rope_tile(proj[:, _Q0:_K0][:, sl], cos, sin) * (HEAD_DIM ** -0.5)
        k_ref[:, sl] = _rope_tile(proj[:, _K0:_V0][:, sl], cos, sin)
        v_ref[:, sl] = proj[:, _V0:_U0][:, sl]

    u = proj[:, _U0:_X0]
    xr = proj[:, _X0:_G0]
    gt = proj[:, _G0:]
    ubuf[:, 0:POOL_MAX, :] = spool_ref[...]
    ubuf[:, POOL_MAX:, :] = u.reshape(nb, SUBLANES, D_B)
    xbuf[:, 0:SUBLANES, :] = sconv_ref[...]
    xbuf[:, SUBLANES:, :] = xr.reshape(nb, SUBLANES, D_C)

    lo = lax.broadcasted_iota(jnp.int32, (n, LANES), 1) < HEAD_DIM
    cnt0 = jnp.where(lo, 2.0, 4.0)
    cnt1 = jnp.where(lo, 8.0, 16.0)
    o_b = _pool_out(lambda s: ubuf[:, pl.ds(POOL_MAX - s, SUBLANES), 0:LANES].reshape(n, LANES),
                    lambda s: ubuf[:, pl.ds(POOL_MAX - s, SUBLANES), LANES:D_B].reshape(n, LANES),
                    u, cnt0, cnt1, wpool_ref[...], pscale_ref[...])

    a, b = _rglru_terms(lambda s: xbuf[:, pl.ds(SUBLANES - s, SUBLANES), :].reshape(n, D_C), convw_ref[...],
                        convb_ref[...], wg_ref[...], bg_ref[...], lam_ref[...])
    a_cum, b_cum = _scan_rows(a, b, SUBLANES, SUBLANES)
    h = a_cum * h0_ref[...] + b_cum
    obc_ref[:, 0:D_B] = o_b
    obc_ref[:, D_B:] = h * jax.nn.gelu(gt)
    npool_ref[...] = ubuf[:, SUBLANES:, :]
    nconv_ref[...] = xr.reshape(nb, SUBLANES, D_C)
    nh_ref[...] = h


def _sample_proj(x, lw, cos, sin, spool, sconv, h0rows):
    n = x.shape[0]
    nb = n // SUBLANES
    shapes = [
        jax.ShapeDtypeStruct((n, D_A), F32), jax.ShapeDtypeStruct((n, D_A), F32), jax.ShapeDtypeStruct((n, D_A), F32),
        jax.ShapeDtypeStruct((n, D_B + D_C), F32),
        jax.ShapeDtypeStruct((nb, POOL_MAX, D_B), F32),
        jax.ShapeDtypeStruct((nb, SUBLANES, D_C), F32),
        jax.ShapeDtypeStruct((n, D_C), F32),
    ]
    args = (x, lw["g1"], lw["w_in"], cos, sin, lw["wpool"], lw["pscale"], lw["convw"], lw["convb"], lw["wg"],
            lw["bg"], lw["lam"], spool, sconv, h0rows)
    return pl.pallas_call(
        _sample_proj_kernel,
        grid=(1,),
        in_specs=[_const_spec(a.shape) for a in args],
        out_specs=[_const_spec(s.shape) for s in shapes],
        out_shape=shapes,
        scratch_shapes=[
            pltpu.VMEM((nb, POOL_MAX + SUBLANES, D_B), F32),
            pltpu.VMEM((nb, 2 * SUBLANES, D_C), F32),
        ],
        compiler_params=pltpu.CompilerParams(dimension_semantics=("arbitrary",), vmem_limit_bytes=VMEM_LIMIT),
        name="sample_proj",
    )(*args)


def _sample_key_counts(w_buf, t_new):
    k = np.arange(w_buf + LANES)[None, :]
    dist = w_buf + np.arange(t_new)[:, None] - k
    cnt = np.zeros(dist.shape, np.float32)
    for window, dil in DILATED_PATTERNS:
        cnt += ((dist >= 0) & (dist <= window) & (dist % dil == 0)).astype(np.float32)
    return np.tile(cnt, (N_HEADS_A, 1))


def _sample_attn_kernel(cnt_ref, ck_ref, cv_ref, q_ref, kn_ref, vn_ref, *rest):
    sk_ref, sv_ref, oa_ref = rest[-3:]
    w_buf = ck_ref.shape[1]
    t_new = q_ref.shape[0]
    ck = ck_ref[...]
    cv = cv_ref[...]
    kn = kn_ref[...]
    vn = vn_ref[...]
    zpad = jnp.zeros((LANES - t_new, D_A), F32)

    tail = lax.broadcasted_iota(jnp.int32, (D_A, LANES), 1) >= LANES - t_new
    for old, new, s_ref in ((ck, kn, sk_ref), (cv, vn, sv_ref)):
        rolled = pltpu.roll(old, w_buf - t_new, axis=1)
        new_t = jnp.concatenate([zpad, new], axis=0).T
        s_ref[:, 0:w_buf - LANES] = rolled[:, 0:w_buf - LANES]
        s_ref[:, w_buf - LANES:] = jnp.where(tail, new_t, rolled[:, w_buf - LANES:])

    qt = jnp.concatenate([q_ref[...]] * N_HEADS_A, axis=0)
    row = lax.broadcasted_iota(jnp.int32, qt.shape, 0)
    lane = lax.broadcasted_iota(jnp.int32, qt.shape, 1)
    own = (row // t_new) == (lane // HEAD_DIM)
    qbd = jnp.where(own, qt, 0.0).astype(BF16)
    kn_pad = jnp.concatenate([kn, zpad], axis=0).astype(BF16)
    vn_pad = jnp.concatenate([vn, zpad], axis=0).astype(BF16)
    s = jnp.concatenate([
        jnp.dot(qbd, ck.astype(BF16), preferred_element_type=F32),
        lax.dot_general(qbd, kn_pad, (((1,), (1,)), ((), ())), preferred_element_type=F32)], axis=1)
    cnt = cnt_ref[...]
    s = jnp.where(cnt > 0.0, s, NEG)
    m = jnp.max(s, axis=1, keepdims=True)
    p = jnp.exp(s - m) * cnt
    den = jnp.sum(p, axis=1, keepdims=True)
    pb = p.astype(BF16)
    o_n = (lax.dot_general(pb[:, :w_buf], cv.astype(BF16), (((1,), (1,)), ((), ())), preferred_element_type=F32)
           + jnp.dot(pb[:, w_buf:], vn_pad, preferred_element_type=F32)) / den
    out = jnp.zeros((t_new, D_A), F32)
    lane_o = lax.broadcasted_iota(jnp.int32, (t_new, D_A), 1) // HEAD_DIM
    for h in range(N_HEADS_A):
        out = jnp.where(lane_o == h, o_n[h * t_new:(h + 1) * t_new, :], out)
    oa_ref[...] = out


def _sample_attn(layer, cache_k, cache_v, q, kn, vn, win_bufs, counts):
    depth, nb, _, w_buf = cache_k.shape
    t_new = q.shape[0] // nb
    cache_spec = pl.BlockSpec((None, None, D_A, w_buf), lambda b: (layer, b, 0, 0))
    tok_spec = pl.BlockSpec((t_new, D_A), lambda b: (b, 0))
    in_specs = [_const_spec(counts.shape), cache_spec, cache_spec, tok_spec, tok_spec, tok_spec]
    args = [counts, cache_k, cache_v, q, kn, vn]
    aliases = {}
    if win_bufs is not None:
        in_specs += [pl.BlockSpec(memory_space=pl.ANY)] * 2
        aliases = {len(args): 0, len(args) + 1: 1}
        args += list(win_bufs)
    win_shape = jax.ShapeDtypeStruct(cache_k.shape, F32)
    sk, sv, oa = pl.pallas_call(
        _sample_attn_kernel,
        grid=(nb,),
        in_specs=in_specs,
        out_specs=[cache_spec, cache_spec, tok_spec],
        out_shape=[win_shape, win_shape, jax.ShapeDtypeStruct(q.shape, F32)],
        input_output_aliases=aliases,
        compiler_params=pltpu.CompilerParams(dimension_semantics=("parallel",), vmem_limit_bytes=VMEM_LIMIT),
        name="sample_attn",
    )(*args)
    return (sk, sv), oa


def _rope_tables(pos):
    half = HEAD_DIM // 2
    inv = jnp.power(ROPE_THETA, -2.0 * jnp.arange(half, dtype=F32) / HEAD_DIM)
    ang = pos.astype(F32)[:, None] * inv[None, :]
    cos = jnp.tile(jnp.cos(ang), (1, LANES // half))
    sin = jnp.sin(ang)
    sin_signed = jnp.tile(jnp.concatenate([-sin, sin], axis=1), (1, LANES // HEAD_DIM))
    return cos, sin_signed


def _block_diag(w):
    n, c, d = w.shape
    eye = jnp.eye(n, dtype=w.dtype)
    return (eye[:, None, :, None] * w[:, :, None, :]).reshape(n * c, n * d)


def _layer_weights(l, norm1_g, w_in, pool_w, pool_scale, conv_w, conv_b, gate_a_w, gate_a_b, gate_x_w, gate_x_b,
                   lru_lambda, w_out, norm2_g, w_gu, w_down):
    return dict(
        g1=norm1_g[l][None, :],
        w_in=w_in[l].astype(BF16),
        wpool=_block_diag(pool_w[l]).astype(BF16),
        pscale=pool_scale[l][None, :],
        convw=conv_w[l],
        convb=conv_b[l][None, :],
        wg=jnp.concatenate([_block_diag(gate_a_w[l]), _block_diag(gate_x_w[l])], axis=1).astype(BF16),
        bg=jnp.concatenate([gate_a_b[l], gate_x_b[l]])[None, :],
        lam=lru_lambda[l][None, :],
        w_out=w_out[l].astype(BF16),
        g2=norm2_g[l][None, :],
        w_gu=w_gu[l].astype(BF16),
        w_down=w_down[l].astype(BF16),
    )


def kernel(x_prompt, x_sample, cache_win_k, cache_win_v, state_pool, state_conv, state_rglru, norm1_g, w_in, pool_w,
           pool_scale, conv_w, conv_b, gate_a_w, gate_a_b, gate_x_w, gate_x_b, lru_lambda, w_out, norm2_g, w_gu,
           w_down, final_g):
    batch, seq, _ = x_prompt.shape
    nb, t_new, _ = x_sample.shape
    depth, _, w_buf = cache_win_k.shape[:3]
    assert batch == 1 and seq % SUPER == 0 and t_new == SUBLANES and w_buf == W_MAX
    tm_proj = 256
    tm_ffn = 512

    cos_p, sin_p = _rope_tables(jnp.arange(seq, dtype=jnp.int32))
    cos_s, sin_s = _rope_tables(PAST_LEN + jnp.arange(t_new, dtype=jnp.int32))
    cos_s = jnp.tile(cos_s, (nb, 1))
    sin_s = jnp.tile(sin_s, (nb, 1))
    counts = jnp.asarray(_sample_key_counts(w_buf, t_new))
    fg = final_g[None, :]

    ck = jnp.transpose(cache_win_k, (0, 1, 3, 4, 2)).reshape(depth, nb, D_A, w_buf)
    cv = jnp.transpose(cache_win_v, (0, 1, 3, 4, 2)).reshape(depth, nb, D_A, w_buf)
    spool = jnp.pad(state_pool, ((0, 0), (0, 0), (1, 0), (0, 0)))
    sconv = jnp.pad(state_conv, ((0, 0), (0, 0), (SUBLANES - (CONV_W - 1), 0), (0, 0)))
    h0rows = jnp.repeat(state_rglru, t_new, axis=1)

    hp = x_prompt.reshape(seq, D_MODEL)
    hs = x_sample.reshape(nb * t_new, D_MODEL)
    win_bufs = None
    p_k, p_v, p_pool, p_conv, p_h, s_pool, s_conv, s_h = [], [], [], [], [], [], [], []
    keep = min(W_MAX, seq)
    for l in range(depth):
        lw = _layer_weights(l, norm1_g, w_in, pool_w, pool_scale, conv_w, conv_b, gate_a_w, gate_a_b, gate_x_w,
                            gate_x_b, lru_lambda, w_out, norm2_g, w_gu, w_down)
        last = l == depth - 1
        q, k, v, obc, pst, cst, hst = _prompt_proj(hp, lw, cos_p, sin_p, tm_proj)
        oa = _prompt_attn(q, k, v)
        hp = _outffn(hp, oa, obc, lw, fg, last, tm_ffn)
        p_k.append(jnp.transpose(k[:, seq - keep:, :], (1, 0, 2)).reshape(1, keep, N_HEADS_A, HEAD_DIM))
        p_v.append(jnp.transpose(v[:, seq - keep:, :], (1, 0, 2)).reshape(1, keep, N_HEADS_A, HEAD_DIM))
        p_pool.append(pst[None, 1:, :])
        p_conv.append(cst[None, SUBLANES - (CONV_W - 1):, :])
        p_h.append(hst[SUBLANES - 1:, :])
        qs, kn, vn, obc_s, npool, nconv, nh = _sample_proj(hs, lw, cos_s, sin_s, spool[l], sconv[l], h0rows[l])
        win_bufs, oa_s = _sample_attn(l, ck, cv, qs, kn, vn, win_bufs, counts)
        hs = _outffn(hs, oa_s, obc_s, lw, fg, last, nb * t_new)
        s_pool.append(npool[:, 1:, :])
        s_conv.append(nconv[:, SUBLANES - (CONV_W - 1):, :])
        s_h.append(nh.reshape(nb, t_new, D_C)[:, t_new - 1, :])

    y_prompt = hp.reshape(batch, seq, D_MODEL)
    y_sample = hs.reshape(nb, t_new, D_MODEL)
    s_win_k = jnp.transpose(win_bufs[0].reshape(depth, nb, N_HEADS_A, HEAD_DIM, w_buf), (0, 1, 4, 2, 3))
    s_win_v = jnp.transpose(win_bufs[1].reshape(depth, nb, N_HEADS_A, HEAD_DIM, w_buf), (0, 1, 4, 2, 3))
    return (y_prompt, y_sample, jnp.stack(p_k), jnp.stack(p_v), jnp.stack(p_pool), jnp.stack(p_conv),
            jnp.stack(p_h), s_win_k, s_win_v, jnp.stack(s_pool), jnp.stack(s_conv), jnp.stack(s_h))
```

```python
import functools

import numpy as np
import jax
import jax.numpy as jnp
from jax import lax
from jax.experimental import pallas as pl
from jax.experimental.pallas import tpu as pltpu

F32 = jnp.float32
BF16 = jnp.bfloat16

D_MODEL = 1024
HEAD_DIM = 64
D_A = 384
N_HEADS_A = D_A // HEAD_DIM
D_B = 256
D_C = 384
D_IN = 3 * D_A + D_B + 2 * D_C
D_FF = 2816
POOL_MAX = 16
CONV_W = 4
RG_C = 8.0
EPS = 1e-6
ROPE_THETA = 10000.0
W_MAX = 2048
PAST_LEN = 16384
DILATED_PATTERNS = ((128, 1), (512, 4), (2048, 16))
BAND = 128
MAX_DIL = 16
SUPER = BAND * MAX_DIL

LANES = 128
SUBLANES = 8
NEG = -1e30
VMEM_LIMIT = 56 * 1024 * 1024

PROJ_TILE = 1024
PROJ_SUB = 256
FFN_TILE = 512
HEAD_SPLIT = 3
D_A_BLK = D_A // HEAD_SPLIT

_Q0, _K0, _V0, _U0, _X0, _G0 = 0, D_A, 2 * D_A, 3 * D_A, 3 * D_A + D_B, 3 * D_A + D_B + D_C


def _rms(x, g):
    return x * lax.rsqrt(jnp.mean(x * x, axis=-1, keepdims=True) + EPS) * g


def _rope_tile(t, cos, sin_signed):
    lane = lax.broadcasted_iota(jnp.int32, t.shape, 1)
    first = (lane % HEAD_DIM) < (HEAD_DIM // 2)
    swapped = jnp.where(first, pltpu.roll(t, LANES - HEAD_DIM // 2, axis=1), pltpu.roll(t, HEAD_DIM // 2, axis=1))
    return t * cos + swapped * sin_signed


def _log_sigmoid(x):
    return jnp.minimum(x, 0.0) - jnp.log1p(jnp.exp(-jnp.abs(x)))


def _project(x, g1, w_in):
    hn = _rms(x, g1)
    return jnp.dot(hn.astype(BF16), w_in, preferred_element_type=F32)


def _pool_map(sums0, sums1, u, cnt0, cnt1, wpool, pscale):
    lo = lax.broadcasted_iota(jnp.int32, cnt0.shape, 1) < HEAD_DIM
    mean0 = jnp.where(lo, sums0[0], sums0[1]) / cnt0
    mean1 = jnp.where(lo, sums1[0], sums1[1]) / cnt1
    pooled = jnp.concatenate([mean0, mean1], axis=1) - u
    return jnp.dot(pooled.astype(BF16), wpool, preferred_element_type=F32) * pscale


def _window_sums_shifted(shift, narrow, wide):
    s = shift(0)
    for i in range(1, narrow):
        s = s + shift(i)
    s_narrow = s
    for i in range(narrow, wide):
        s = s + shift(i)
    return s_narrow, s


def _window_sums_rolled(xe, halo):
    s2 = xe + pltpu.roll(xe, 1, axis=0)
    s4 = s2 + pltpu.roll(s2, 2, axis=0)
    s8 = s4 + pltpu.roll(s4, 4, axis=0)
    s16 = s8 + pltpu.roll(s8, 8, axis=0)
    return s2[halo:], s4[halo:], s8[halo:], s16[halo:]


def _rglru_terms(xshift, convw, convb, wg, bg, lam):
    xc = convb + (xshift(3) * convw[0:1] + xshift(2) * convw[1:2] + xshift(1) * convw[2:3] + xshift(0) * convw[3:4])
    g = jnp.dot(xc.astype(BF16), wg, preferred_element_type=F32) + bg
    r = jax.nn.sigmoid(g[:, :D_C])
    ig = jax.nn.sigmoid(g[:, D_C:])
    half_log_a = (0.5 * RG_C) * r * _log_sigmoid(lam)
    a = jnp.exp(2.0 * half_log_a)
    b = (1.0 + a) * jnp.sqrt(-jnp.tanh(half_log_a)) * ig * xc
    return a, b


def _compose_step(a, b, shift):
    b = a * pltpu.roll(b, shift, axis=0) + b
    a = a * pltpu.roll(a, shift, axis=0)
    return a, b


def _scan_groups(a, b, h0_rows):
    row = lax.broadcasted_iota(jnp.int32, a.shape, 0) % SUBLANES
    for s in (1, 2, 4):
        keep = row >= s
        a_sh = jnp.where(keep, pltpu.roll(a, s, axis=0), 1.0)
        b_sh = jnp.where(keep, pltpu.roll(b, s, axis=0), 0.0)
        b = a * b_sh + b
        a = a * a_sh
    return a * h0_rows + b


def _scan_tile(a, b, h_last):
    n, c = a.shape
    a = jnp.concatenate([jnp.ones((SUBLANES, c), F32), a], axis=0)
    b = jnp.concatenate([jnp.zeros((SUBLANES, c), F32), b], axis=0)
    for s in (1, 2, 4):
        a, b = _compose_step(a, b, s)
    h = jnp.broadcast_to(h_last, (SUBLANES, c))
    out = []
    for j in range(1, n // SUBLANES + 1):
        rows = slice(j * SUBLANES, (j + 1) * SUBLANES)
        h = a[rows] * h + b[rows]
        out.append(h)
    return jnp.concatenate(out, axis=0)


def _prompt_proj_kernel(x_ref, g1_ref, win_ref, cr_ref, sr_ref, crs_ref, srs_ref, cb_ref, sb_ref,
                        wpool_ref, pscale_ref, convw_ref, convb_ref, wg_ref, bg_ref, lam_ref,
                        q_ref, k_ref, v_ref, obc_ref, pst_ref, cst_ref, hst_ref,
                        ubuf, xbuf, hcar):
    i = pl.program_id(0)
    tm = x_ref.shape[0]

    @pl.when(i == 0)
    def _():
        ubuf[0:POOL_MAX, :] = jnp.zeros((POOL_MAX, D_B), F32)
        xbuf[0:SUBLANES, :] = jnp.zeros((SUBLANES, D_C), F32)
        hcar[...] = jnp.zeros((SUBLANES, D_C), F32)

    h_last = hcar[SUBLANES - 1:SUBLANES, :]
    for r0 in range(0, tm, PROJ_SUB):
        rows = slice(r0, r0 + PROJ_SUB)
        proj = _project(x_ref[rows, :], g1_ref[...], win_ref[...])
        sub = i * (tm // PROJ_SUB) + r0 // PROJ_SUB
        cb = cb_ref[pl.ds(sub, 1), :]
        sb = sb_ref[pl.ds(sub, 1), :]
        cos = cb * cr_ref[...] - sb * sr_ref[...]
        sin = sb * crs_ref[...] + cb * srs_ref[...]
        for hp in range(D_A // LANES):
            sl = slice(hp * LANES, (hp + 1) * LANES)
            q_ref[hp, rows, :] = _rope_tile(proj[:, _Q0:_K0][:, sl], cos, sin)
            k_ref[hp, rows, :] = _rope_tile(proj[:, _K0:_V0][:, sl], cos, sin)
            v_ref[hp, rows, :] = proj[:, _V0:_U0][:, sl]

        u = proj[:, _U0:_X0]
        xr = proj[:, _X0:_G0]
        gt = proj[:, _G0:]
        ubuf[POOL_MAX + r0:POOL_MAX + r0 + PROJ_SUB, :] = u
        xbuf[SUBLANES + r0:SUBLANES + r0 + PROJ_SUB, :] = xr

        pos1 = (i * tm + r0 + 1 + lax.broadcasted_iota(jnp.int32, (PROJ_SUB, LANES), 0)).astype(F32)
        lo = lax.broadcasted_iota(jnp.int32, (PROJ_SUB, LANES), 1) < HEAD_DIM
        cnt0 = jnp.minimum(pos1, jnp.where(lo, 2.0, 4.0))
        cnt1 = jnp.minimum(pos1, jnp.where(lo, 8.0, 16.0))
        ext = slice(r0, r0 + PROJ_SUB + POOL_MAX)
        s2, s4, _, _ = _window_sums_rolled(ubuf[ext, 0:LANES], POOL_MAX)
        _, _, s8, s16 = _window_sums_rolled(ubuf[ext, LANES:D_B], POOL_MAX)
        o_b = _pool_map((s2, s4), (s8, s16), u, cnt0, cnt1, wpool_ref[...], pscale_ref[...])

        a, b = _rglru_terms(lambda s, r0=r0: xbuf[pl.ds(SUBLANES + r0 - s, PROJ_SUB), :], convw_ref[...],
                            convb_ref[...], wg_ref[...], bg_ref[...], lam_ref[...])
        h = _scan_tile(a, b, h_last)
        h_last = h[PROJ_SUB - 1:, :]
        obc_ref[rows, 0:D_B] = o_b
        obc_ref[rows, D_B:] = h * jax.nn.gelu(gt)

    ubuf[0:POOL_MAX, :] = u[PROJ_SUB - POOL_MAX:, :]
    xbuf[0:SUBLANES, :] = xr[PROJ_SUB - SUBLANES:, :]
    hcar[...] = h[PROJ_SUB - SUBLANES:, :]
    pst_ref[...] = u[PROJ_SUB - POOL_MAX:, :]
    cst_ref[...] = xr[PROJ_SUB - SUBLANES:, :]
    hst_ref[...] = h[PROJ_SUB - SUBLANES:, :]


def _const_spec(shape):
    return pl.BlockSpec(shape, lambda *_: (0,) * len(shape))


def _layer_spec(layer, shape):
    return pl.BlockSpec((None,) + tuple(shape), lambda *_: (layer,) + (0,) * len(shape),
                        pipeline_mode=pl.Buffered(1))


def _prompt_proj(x, layer, pw, rope, tm):
    t = x.shape[0]
    n_hp = D_A // LANES
    qkv_shape = jax.ShapeDtypeStruct((n_hp, t, LANES), F32)
    qkv_spec = pl.BlockSpec((n_hp, tm, LANES), lambda i: (0, i, 0))
    return pl.pallas_call(
        _prompt_proj_kernel,
        grid=(t // tm,),
        in_specs=([pl.BlockSpec((tm, D_MODEL), lambda i: (i, 0))]
                  + [_layer_spec(layer, pw[k].shape[1:]) for k in _PROJ_PARAMS]
                  + [_const_spec(r.shape) for r in rope]
                  + [_layer_spec(layer, pw[k].shape[1:]) for k in _MIX_PARAMS]),
        out_specs=[
            qkv_spec, qkv_spec, qkv_spec,
            pl.BlockSpec((tm, D_B + D_C), lambda i: (i, 0)),
            _const_spec((POOL_MAX, D_B)),
            _const_spec((SUBLANES, D_C)),
            _const_spec((SUBLANES, D_C)),
        ],
        out_shape=[
            qkv_shape, qkv_shape, qkv_shape,
            jax.ShapeDtypeStruct((t, D_B + D_C), F32),
            jax.ShapeDtypeStruct((POOL_MAX, D_B), F32),
            jax.ShapeDtypeStruct((SUBLANES, D_C), F32),
            jax.ShapeDtypeStruct((SUBLANES, D_C), F32),
        ],
        scratch_shapes=[
            pltpu.VMEM((POOL_MAX + tm, D_B), F32),
            pltpu.VMEM((SUBLANES + tm, D_C), F32),
            pltpu.VMEM((SUBLANES, D_C), F32),
        ],
        compiler_params=pltpu.CompilerParams(dimension_semantics=("arbitrary",), vmem_limit_bytes=VMEM_LIMIT),
        name="prompt_proj",
    )(x, *[pw[k] for k in _PROJ_PARAMS], *rope, *[pw[k] for k in _MIX_PARAMS])


def _band_bias(dil):
    slabs = MAX_DIL // dil
    rows = BAND // slabs
    row = lax.broadcasted_iota(jnp.int32, (BAND, 2 * BAND), 0)
    col = lax.broadcasted_iota(jnp.int32, (BAND, 2 * BAND), 1)
    qi = slabs * (row % rows) + row // rows
    ki = slabs * (col % (2 * rows)) + col // (2 * rows)
    valid = (ki >= qi) & (ki <= qi + BAND)
    bias = jnp.where(valid, 0.0, NEG)
    bias_first = jnp.where(valid & (ki >= BAND), 0.0, NEG)
    return bias, bias_first


def _prompt_attn_kernel(q_ref, k_ref, v_ref, o_ref, kt_ref, vt_ref, q16, kcat, vcat, acc_s, m_s, l_s):
    i = pl.program_id(1)

    @pl.when(i == pl.num_programs(1) - 1)
    def _():
        kt_ref[...] = k_ref[...].T
        vt_ref[...] = v_ref[...].T

    @pl.when(i == 0)
    def _():
        kcat[0:BAND, :] = jnp.zeros((BAND, MAX_DIL * LANES), F32)
        vcat[0:BAND, :] = jnp.zeros((BAND, MAX_DIL * LANES), F32)

    @pl.when(i > 0)
    def _():
        kcat[0:BAND, :] = kcat[BAND:, :]
        vcat[0:BAND, :] = vcat[BAND:, :]

    for r in range(MAX_DIL):
        ls = slice(r * LANES, (r + 1) * LANES)
        q16[:, ls] = q_ref[pl.ds(r, BAND, stride=MAX_DIL), :]
        kcat[BAND:, ls] = k_ref[pl.ds(r, BAND, stride=MAX_DIL), :]
        vcat[BAND:, ls] = v_ref[pl.ds(r, BAND, stride=MAX_DIL), :]

    head0 = lax.broadcasted_iota(jnp.int32, (BAND, LANES), 1) < HEAD_DIM
    ones_cols = jnp.ones((2 * BAND, LANES), BF16)

    for stage, (_, dil) in enumerate(DILATED_PATTERNS):
        slabs = MAX_DIL // dil
        rows = BAND // slabs
        bias, bias_first = _band_bias(dil)
        bias_blk0 = jnp.where(i == 0, bias_first, bias)
        for blk in range(slabs):
            for res in range(dil):
                groups = [res + dil * c for c in range(slabs)]
                q_rows = slice(rows * blk, rows * (blk + 1))
                k_rows = slice(BAND + rows * (blk - 1), BAND + rows * (blk + 1))
                lanes = [slice(g * LANES, (g + 1) * LANES) for g in groups]
                qb = jnp.concatenate([q16[q_rows, ls] for ls in lanes], axis=0)
                kk = jnp.concatenate([kcat[k_rows, ls] for ls in lanes], axis=0).astype(BF16)
                vv = jnp.concatenate([vcat[k_rows, ls] for ls in lanes], axis=0).astype(BF16)
                vv1 = jnp.concatenate([vv, ones_cols], axis=1)
                b_u = bias_blk0 if blk == 0 else bias
                q2 = jnp.concatenate([jnp.where(head0, qb, 0.0), jnp.where(head0, 0.0, qb)], axis=0).astype(BF16)
                s2 = lax.dot_general(q2, kk, (((1,), (1,)), ((), ())), preferred_element_type=F32)
                ms, ps = [], []
                for hh in range(2):
                    s = s2[hh * BAND:(hh + 1) * BAND] + b_u
                    m = jnp.max(s, axis=1, keepdims=True)
                    ms.append(m)
                    ps.append(jnp.exp((s - m).astype(BF16)))
                pv2 = jnp.dot(jnp.concatenate(ps, axis=0), vv1, preferred_element_type=F32)
                parts = [(ms[hh], pv2[hh * BAND:(hh + 1) * BAND, LANES:], pv2[hh * BAND:(hh + 1) * BAND, :LANES])
                         for hh in range(2)]
                m_b = jnp.where(head0, parts[0][0], parts[1][0])
                l_b = jnp.where(head0, parts[0][1], parts[1][1])
                pv_b = jnp.where(head0, parts[0][2], parts[1][2])
                if stage > 0:
                    m_o = jnp.concatenate([m_s[q_rows, ls] for ls in lanes], axis=0)
                    l_o = jnp.concatenate([l_s[q_rows, ls] for ls in lanes], axis=0)
                    a_o = jnp.concatenate([acc_s[q_rows, ls] for ls in lanes], axis=0)
                    m_n = jnp.maximum(m_o, m_b)
                    w_o = jnp.exp(m_o - m_n)
                    w_b = jnp.exp(m_b - m_n)
                    l_b = l_o * w_o + l_b * w_b
                    pv_b = a_o * w_o + pv_b * w_b
                    m_b = m_n
                last = stage == len(DILATED_PATTERNS) - 1
                if last:
                    out = pv_b / l_b
                for c, ls in enumerate(lanes):
                    sub = slice(c * rows, (c + 1) * rows)
                    if last:
                        o_ref[pl.ds(MAX_DIL * rows * blk + groups[c], rows, stride=MAX_DIL), :] = out[sub]
                    else:
                        m_s[q_rows, ls] = m_b[sub]
                        l_s[q_rows, ls] = l_b[sub]
                        acc_s[q_rows, ls] = pv_b[sub]


def _prompt_attn(q, k, v):
    n_hp, t, _ = q.shape
    n_super = t // SUPER
    width = MAX_DIL * LANES
    cur = pl.BlockSpec((None, SUPER, LANES), lambda h, i: (h, i, 0))
    win = pl.BlockSpec((None, LANES, SUPER), lambda h, i: (h, 0, 0))
    win_shape = jax.ShapeDtypeStruct((n_hp, LANES, SUPER), F32)
    return pl.pallas_call(
        _prompt_attn_kernel,
        grid=(n_hp, n_super),
        in_specs=[cur, cur, cur],
        out_specs=[cur, win, win],
        out_shape=[jax.ShapeDtypeStruct(q.shape, F32), win_shape, win_shape],
        scratch_shapes=[
            pltpu.VMEM((BAND, width), F32),
            pltpu.VMEM((2 * BAND, width), F32),
            pltpu.VMEM((2 * BAND, width), F32),
            pltpu.VMEM((BAND, width), F32),
            pltpu.VMEM((BAND, width), F32),
            pltpu.VMEM((BAND, width), F32),
        ],
        compiler_params=pltpu.CompilerParams(dimension_semantics=("arbitrary", "arbitrary"),
                                             vmem_limit_bytes=VMEM_LIMIT),
        name="prompt_attn",
    )(q, k, v)


FF_CHUNK = 256


def _outffn_begin(x_ref, oa_ref, obc_ref, wout_ref, g2_ref):
    mix = jnp.concatenate([oa_ref[c] for c in range(oa_ref.shape[0])] + [obc_ref[...]], axis=1).astype(BF16)
    x1 = x_ref[...] + jnp.dot(mix, wout_ref[...], preferred_element_type=F32)
    return x1, _rms(x1, g2_ref[...]).astype(BF16)


def _ffn_chunks(h2, acc, wgu_ref, wdown_ref, chunks):
    for c in chunks:
        g = jnp.dot(h2, wgu_ref[:, c * FF_CHUNK:(c + 1) * FF_CHUNK], preferred_element_type=F32)
        up = jnp.dot(h2, wgu_ref[:, D_FF + c * FF_CHUNK:D_FF + (c + 1) * FF_CHUNK], preferred_element_type=F32)
        act = (jax.nn.silu(g) * up).astype(BF16)
        acc = acc + jnp.dot(act, wdown_ref[c * FF_CHUNK:(c + 1) * FF_CHUNK, :], preferred_element_type=F32)
    return acc


def _outffn_rows(x_ref, oa_ref, obc_ref, wout_ref, g2_ref, wgu_ref, wdown_ref, fg_ref, final):
    x1, h2 = _outffn_begin(x_ref, oa_ref, obc_ref, wout_ref, g2_ref)
    acc = _ffn_chunks(h2, x1, wgu_ref, wdown_ref, range(D_FF // FF_CHUNK))
    return _rms(acc, fg_ref[...]) if final else acc


def _outffn_kernel(x_ref, oa_ref, obc_ref, wout_ref, g2_ref, wgu_ref, wdown_ref, fg_ref, o_ref, *, final):
    o_ref[...] = _outffn_rows(x_ref, oa_ref, obc_ref, wout_ref, g2_ref, wgu_ref, wdown_ref, fg_ref, final)


def _outffn_specs(oa, layer, pw, tm, index):
    return [
        pl.BlockSpec((tm, D_MODEL), lambda *g: (index(*g), 0)),
        pl.BlockSpec((oa.shape[0], tm, oa.shape[2]), lambda *g: (0, index(*g), 0)),
        pl.BlockSpec((tm, D_B + D_C), lambda *g: (index(*g), 0)),
    ] + [_layer_spec(layer, pw[k].shape[1:]) for k in _FFN_PARAMS] + [_const_spec((1, D_MODEL))]


def _outffn(x, oa, obc, layer, pw, final_g, final, tm):
    t = x.shape[0]
    return pl.pallas_call(
        functools.partial(_outffn_kernel, final=final),
        grid=(t // tm,),
        in_specs=_outffn_specs(oa, layer, pw, tm, lambda i: i),
        out_specs=pl.BlockSpec((tm, D_MODEL), lambda i: (i, 0)),
        out_shape=jax.ShapeDtypeStruct((t, D_MODEL), F32),
        compiler_params=pltpu.CompilerParams(dimension_semantics=("parallel",), vmem_limit_bytes=VMEM_LIMIT),
        name="outffn",
    )(x, oa, obc, *[pw[k] for k in _FFN_PARAMS], final_g)


def _sample_proj_kernel(x_ref, g1_ref, win_ref, cos_ref, sin_ref, wpool_ref, pscale_ref, convw_ref, convb_ref,
                        wg_ref, bg_ref, lam_ref, spool_ref, sconv_ref, h0_ref,
                        q_ref, k_ref, v_ref, obc_ref, npool_ref, nconv_ref, nh_ref,
                        ubuf, xbuf):
    n = x_ref.shape[0]
    nb = n // SUBLANES
    proj = _project(x_ref[...], g1_ref[...], win_ref[...])
    cos = cos_ref[...]
    sin = sin_ref[...]
    for c in range(HEAD_SPLIT):
        sl = slice(c * D_A_BLK, (c + 1) * D_A_BLK)
        q_ref[c] = _rope_tile(proj[:, _Q0:_K0][:, sl], cos, sin)
        k_ref[c] = _rope_tile(proj[:, _K0:_V0][:, sl], cos, sin)
        v_ref[c] = proj[:, _V0:_U0][:, sl]

    u = proj[:, _U0:_X0]
    xr = proj[:, _X0:_G0]
    gt = proj[:, _G0:]
    ubuf[:, 0:POOL_MAX, :] = spool_ref[...]
    ubuf[:, POOL_MAX:, :] = u.reshape(nb, SUBLANES, D_B)
    xbuf[:, 0:SUBLANES, :] = sconv_ref[...]
    xbuf[:, SUBLANES:, :] = xr.reshape(nb, SUBLANES, D_C)

    lo = lax.broadcasted_iota(jnp.int32, (n, LANES), 1) < HEAD_DIM
    cnt0 = jnp.where(lo, 2.0, 4.0)
    cnt1 = jnp.where(lo, 8.0, 16.0)
    sums0 = _window_sums_shifted(lambda s: ubuf[:, pl.ds(POOL_MAX - s, SUBLANES), 0:LANES].reshape(n, LANES), 2, 4)
    sums1 = _window_sums_shifted(lambda s: ubuf[:, pl.ds(POOL_MAX - s, SUBLANES), LANES:D_B].reshape(n, LANES), 8, 16)
    o_b = _pool_map(sums0, sums1, u, cnt0, cnt1, wpool_ref[...], pscale_ref[...])

    a, b = _rglru_terms(lambda s: xbuf[:, pl.ds(SUBLANES - s, SUBLANES), :].reshape(n, D_C), convw_ref[...],
                        convb_ref[...], wg_ref[...], bg_ref[...], lam_ref[...])
    h = _scan_groups(a, b, h0_ref[...])
    obc_ref[:, 0:D_B] = o_b
    obc_ref[:, D_B:] = h * jax.nn.gelu(gt)
    npool_ref[...] = ubuf[:, SUBLANES:, :]
    nconv_ref[...] = xr.reshape(nb, SUBLANES, D_C)
    nh_ref[...] = h


def _sample_proj(x, layer, pw, cos, sin, spool, sconv, h0rows):
    n = x.shape[0]
    nb = n // SUBLANES
    shapes = [
        jax.ShapeDtypeStruct((HEAD_SPLIT, n, D_A_BLK), F32), jax.ShapeDtypeStruct((HEAD_SPLIT, n, D_A_BLK), F32),
        jax.ShapeDtypeStruct((HEAD_SPLIT, n, D_A_BLK), F32),
        jax.ShapeDtypeStruct((n, D_B + D_C), F32),
        jax.ShapeDtypeStruct((nb, POOL_MAX, D_B), F32),
        jax.ShapeDtypeStruct((nb, SUBLANES, D_C), F32),
        jax.ShapeDtypeStruct((n, D_C), F32),
    ]
    data = (spool, sconv, h0rows)
    in_specs = ([_const_spec(x.shape)] + [_layer_spec(layer, pw[k].shape[1:]) for k in _PROJ_PARAMS]
                + [_const_spec(cos.shape), _const_spec(sin.shape)]
                + [_layer_spec(layer, pw[k].shape[1:]) for k in _MIX_PARAMS] + [_const_spec(a.shape) for a in data])
    args = (x, *[pw[k] for k in _PROJ_PARAMS], cos, sin, *[pw[k] for k in _MIX_PARAMS], *data)
    return pl.pallas_call(
        _sample_proj_kernel,
        grid=(1,),
        in_specs=in_specs,
        out_specs=[_const_spec(s.shape) for s in shapes],
        out_shape=shapes,
        scratch_shapes=[
            pltpu.VMEM((nb, POOL_MAX + SUBLANES, D_B), F32),
            pltpu.VMEM((nb, 2 * SUBLANES, D_C), F32),
        ],
        compiler_params=pltpu.CompilerParams(dimension_semantics=("arbitrary",), vmem_limit_bytes=VMEM_LIMIT),
        name="sample_proj",
    )(*args)


def _sample_key_counts(w_buf, t_new, n_heads):
    k = np.arange(w_buf + LANES)[None, :]
    dist = w_buf + np.arange(t_new)[:, None] - k
    cnt = np.zeros(dist.shape, np.float32)
    for window, dil in DILATED_PATTERNS:
        cnt += ((dist >= 0) & (dist <= window) & (dist % dil == 0)).astype(np.float32)
    return np.tile(cnt, (n_heads, 1))


def _shift_window(c_ref, new_ref, s_ref):
    da, w_buf = c_ref.shape
    t_new = new_ref.shape[0]
    tail = lax.broadcasted_iota(jnp.int32, (da, LANES), 1) >= LANES - t_new
    rolled = pltpu.roll(c_ref[...], w_buf - t_new, axis=1)
    new_t = jnp.concatenate([jnp.zeros((LANES - t_new, da), F32), new_ref[...]], axis=0).T
    s_ref[:, 0:w_buf - LANES] = rolled[:, 0:w_buf - LANES]
    s_ref[:, w_buf - LANES:] = jnp.where(tail, new_t, rolled[:, w_buf - LANES:])


def _sample_attn_scores(ck_ref, q_ref, kn_ref):
    da = ck_ref.shape[0]
    t_new = q_ref.shape[0]
    qt = jnp.concatenate([q_ref[...]] * (da // HEAD_DIM), axis=0)
    row = lax.broadcasted_iota(jnp.int32, qt.shape, 0)
    lane = lax.broadcasted_iota(jnp.int32, qt.shape, 1)
    own = (row // t_new) == (lane // HEAD_DIM)
    qbd = jnp.where(own, qt, 0.0).astype(BF16)
    kn_pad = jnp.concatenate([kn_ref[...], jnp.zeros((LANES - t_new, da), F32)], axis=0).astype(BF16)
    return jnp.concatenate([
        jnp.dot(qbd, ck_ref[...].astype(BF16), preferred_element_type=F32),
        lax.dot_general(qbd, kn_pad, (((1,), (1,)), ((), ())), preferred_element_type=F32)], axis=1)


def _sample_attn_finish(s, cnt_ref, cv_ref, vn_ref, oa_ref):
    da, w_buf = cv_ref.shape
    t_new = vn_ref.shape[0]
    cv = cv_ref[...]
    vn = vn_ref[...]
    zpad = jnp.zeros((LANES - t_new, da), F32)

    cnt = cnt_ref[...]
    s = jnp.where(cnt > 0.0, s, NEG)
    m = jnp.max(s, axis=1, keepdims=True)
    p = jnp.exp(s - m) * cnt
    den = jnp.sum(p, axis=1, keepdims=True)
    pb = p.astype(BF16)
    vn_pad = jnp.concatenate([vn, zpad], axis=0).astype(BF16)
    o_n = (lax.dot_general(pb[:, :w_buf], cv.astype(BF16), (((1,), (1,)), ((), ())), preferred_element_type=F32)
           + jnp.dot(pb[:, w_buf:], vn_pad, preferred_element_type=F32)) / den
    out = jnp.zeros((t_new, da), F32)
    lane_o = lax.broadcasted_iota(jnp.int32, (t_new, da), 1) // HEAD_DIM
    for h in range(da // HEAD_DIM):
        out = jnp.where(lane_o == h, o_n[h * t_new:(h + 1) * t_new, :], out)
    oa_ref[...] = out


def _outffn_sample_attn_kernel(x_ref, oa_ref, obc_ref, wout_ref, g2_ref, wgu_ref, wdown_ref, fg_ref,
                               cnt_ref, ck_ref, cv_ref, q_ref, kn_ref, vn_ref, *rest, final, parts):
    o_ref, sk_ref, sv_ref, oas_ref, acc_scr, h2_scr = rest[-6:]

    for part, chunks in enumerate(parts):
        @pl.when(pl.program_id(1) == part)
        def _(part=part, chunks=chunks):
            scores = _sample_attn_scores(ck_ref, q_ref, kn_ref)
            _shift_window(ck_ref, kn_ref, sk_ref)
            _shift_window(cv_ref, vn_ref, sv_ref)
            if part == 0:
                acc, h2 = _outffn_begin(x_ref, oa_ref, obc_ref, wout_ref, g2_ref)
                h2_scr[...] = h2
            else:
                acc, h2 = acc_scr[...], h2_scr[...]
            acc = _ffn_chunks(h2, acc, wgu_ref, wdown_ref, chunks)
            if part == len(parts) - 1:
                o_ref[...] = _rms(acc, fg_ref[...]) if final else acc
            else:
                acc_scr[...] = acc
            _sample_attn_finish(scores, cnt_ref, cv_ref, vn_ref, oas_ref)


def _outffn_sample_attn(x, oa, obc, layer, pw, final_g, final, tm, cache_k, cache_v, q, kn, vn, win_bufs, counts):
    t = x.shape[0]
    n_tiles = t // tm
    depth, nb, split, da, w_buf = cache_k.shape
    t_new = q.shape[1] // nb
    per_tile = nb * split // n_tiles
    assert per_tile * n_tiles == nb * split
    parts = [tuple(int(c) for c in p) for p in np.array_split(np.arange(D_FF // FF_CHUNK), per_tile)[::-1]]

    def group(i, j):
        g = i * per_tile + j
        return g // split, g % split

    def cache_index(i, j):
        b, c = group(i, j)
        return (layer, b, c, 0, 0)

    def tok_index(i, j):
        b, c = group(i, j)
        return (c, b, 0)

    cache_spec = pl.BlockSpec((None, None, None, da, w_buf), cache_index)
    tok_spec = pl.BlockSpec((None, t_new, da), tok_index)
    in_specs = _outffn_specs(oa, layer, pw, tm, lambda i, j: i) + [
        _const_spec(counts.shape), cache_spec, cache_spec, tok_spec, tok_spec, tok_spec]
    args = [x, oa, obc, *[pw[k] for k in _FFN_PARAMS], final_g, counts, cache_k, cache_v, q, kn, vn]
    aliases = {}
    if win_bufs is not None:
        in_specs += [pl.BlockSpec(memory_space=pl.ANY)] * 2
        aliases = {len(args): 1, len(args) + 1: 2}
        args += list(win_bufs)
    win_shape = jax.ShapeDtypeStruct(cache_k.shape, F32)
    y, sk, sv, oa_s = pl.pallas_call(
        functools.partial(_outffn_sample_attn_kernel, final=final, parts=parts),
        grid=(n_tiles, per_tile),
        in_specs=in_specs,
        out_specs=[pl.BlockSpec((tm, D_MODEL), lambda i, j: (i, 0)), cache_spec, cache_spec, tok_spec],
        out_shape=[jax.ShapeDtypeStruct((t, D_MODEL), F32), win_shape, win_shape, jax.ShapeDtypeStruct(q.shape, F32)],
        input_output_aliases=aliases,
        scratch_shapes=[pltpu.VMEM((tm, D_MODEL), F32), pltpu.VMEM((tm, D_MODEL), BF16)],
        compiler_params=pltpu.CompilerParams(dimension_semantics=("arbitrary", "arbitrary"),
                                             vmem_limit_bytes=VMEM_LIMIT),
        name="outffn_sample_attn",
    )(*args)
    return y, (sk, sv), oa_s


def _rope_tables(pos):
    half = HEAD_DIM // 2
    inv = jnp.power(ROPE_THETA, -2.0 * jnp.arange(half, dtype=F32) / HEAD_DIM)
    ang = pos.astype(F32)[:, None] * inv[None, :]
    cos = jnp.tile(jnp.cos(ang), (1, LANES // half))
    sin = jnp.sin(ang)
    sin_signed = jnp.tile(jnp.concatenate([-sin, sin], axis=1), (1, LANES // HEAD_DIM))
    return cos, sin_signed


def _rope_tile_tables(seq):
    half = HEAD_DIM // 2
    inv = jnp.power(ROPE_THETA, -2.0 * jnp.arange(half, dtype=F32) / HEAD_DIM)
    inv = jnp.tile(inv, LANES // half)[None, :]
    sign = jnp.tile(jnp.concatenate([-jnp.ones(half, F32), jnp.ones(half, F32)]), LANES // HEAD_DIM)[None, :]
    ang_r = jnp.arange(PROJ_SUB, dtype=F32)[:, None] * inv
    ang_b = (jnp.arange(seq // PROJ_SUB, dtype=F32) * PROJ_SUB)[:, None] * inv
    cr, sr = jnp.cos(ang_r), jnp.sin(ang_r)
    return cr, sr, sign * cr, sign * sr, jnp.cos(ang_b), jnp.sin(ang_b)


def _block_diag(w):
    depth, n, c, d = w.shape
    eye = jnp.eye(n, dtype=w.dtype)
    return (eye[None, :, None, :, None] * w[:, :, :, None, :]).reshape(depth, n * c, n * d)


_PROJ_PARAMS = ("g1", "w_in")
_MIX_PARAMS = ("wpool", "pscale", "convw", "convb", "wg", "bg", "lam")
_FFN_PARAMS = ("w_out", "g2", "w_gu", "w_down")


def _stack_params(norm1_g, w_in, pool_w, pool_scale, conv_w, conv_b, gate_a_w, gate_a_b, gate_x_w, gate_x_b, lru_lambda,
                  w_out, norm2_g, w_gu, w_down):
    q_scale = jnp.where(jnp.arange(D_IN) < D_A, HEAD_DIM ** -0.5, 1.0).astype(F32)
    return dict(
        g1=norm1_g[:, None, :],
        w_in=(w_in * q_scale).astype(BF16),
        wpool=_block_diag(pool_w).astype(BF16),
        pscale=pool_scale[:, None, :],
        convw=conv_w,
        convb=conv_b[:, None, :],
        wg=jnp.concatenate([_block_diag(gate_a_w), _block_diag(gate_x_w)], axis=2).astype(BF16),
        bg=jnp.concatenate([gate_a_b, gate_x_b], axis=1)[:, None, :],
        lam=lru_lambda[:, None, :],
        w_out=w_out.astype(BF16),
        g2=norm2_g[:, None, :],
        w_gu=w_gu.astype(BF16),
        w_down=w_down.astype(BF16),
    )


def kernel(x_prompt, x_sample, cache_win_k, cache_win_v, state_pool, state_conv, state_rglru, norm1_g, w_in, pool_w,
           pool_scale, conv_w, conv_b, gate_a_w, gate_a_b, gate_x_w, gate_x_b, lru_lambda, w_out, norm2_g, w_gu,
           w_down, final_g):
    batch, seq, _ = x_prompt.shape
    nb, t_new, _ = x_sample.shape
    depth, _, w_buf = cache_win_k.shape[:3]
    assert batch == 1 and seq % SUPER == 0 and seq % PROJ_TILE == 0 and seq % FFN_TILE == 0
    assert t_new == SUBLANES and w_buf == W_MAX and SUPER == W_MAX

    rope_p = _rope_tile_tables(seq)
    cos_s, sin_s = _rope_tables(PAST_LEN + jnp.arange(t_new, dtype=jnp.int32))
    cos_s = jnp.tile(cos_s, (nb, 1))
    sin_s = jnp.tile(sin_s, (nb, 1))
    counts = jnp.asarray(_sample_key_counts(w_buf, t_new, D_A_BLK // HEAD_DIM))
    fg = final_g[None, :]

    ck = jnp.transpose(cache_win_k, (0, 1, 3, 4, 2)).reshape(depth, nb, HEAD_SPLIT, D_A_BLK, w_buf)
    cv = jnp.transpose(cache_win_v, (0, 1, 3, 4, 2)).reshape(depth, nb, HEAD_SPLIT, D_A_BLK, w_buf)
    spool = jnp.pad(state_pool, ((0, 0), (0, 0), (1, 0), (0, 0)))
    sconv = jnp.pad(state_conv, ((0, 0), (0, 0), (SUBLANES - (CONV_W - 1), 0), (0, 0)))
    h0rows = jnp.repeat(state_rglru, t_new, axis=1)

    hp = x_prompt.reshape(seq, D_MODEL)
    hs = x_sample.reshape(nb * t_new, D_MODEL)
    pw = _stack_params(norm1_g, w_in, pool_w, pool_scale, conv_w, conv_b, gate_a_w, gate_a_b, gate_x_w, gate_x_b,
                       lru_lambda, w_out, norm2_g, w_gu, w_down)
    win_bufs = None
    p_k, p_v, p_pool, p_conv, p_h, s_pool, s_conv, s_h = [], [], [], [], [], [], [], []
    for l in range(depth):
        last = l == depth - 1
        qs, kn, vn, obc_s, npool, nconv, nh = _sample_proj(hs, l, pw, cos_s, sin_s, spool[l], sconv[l], h0rows[l])
        q, k, v, obc, pst, cst, hst = _prompt_proj(hp, l, pw, rope_p, PROJ_TILE)
        oa, kt, vt = _prompt_attn(q, k, v)
        hp, win_bufs, oa_s = _outffn_sample_attn(hp, oa, obc, l, pw, fg, last, FFN_TILE, ck, cv, qs, kn, vn,
                                                 win_bufs, counts)
        p_k.append(kt)
        p_v.append(vt)
        p_pool.append(pst[None, 1:, :])
        p_conv.append(cst[None, SUBLANES - (CONV_W - 1):, :])
        p_h.append(hst[SUBLANES - 1:, :])
        hs = _outffn(hs, oa_s, obc_s, l, pw, fg, last, nb * t_new)
        s_pool.append(npool[:, 1:, :])
        s_conv.append(nconv[:, SUBLANES - (CONV_W - 1):, :])
        s_h.append(nh.reshape(nb, t_new, D_C)[:, t_new - 1, :])

    y_prompt = hp.reshape(batch, seq, D_MODEL)
    y_sample = hs.reshape(nb, t_new, D_MODEL)
    s_win_k = jnp.transpose(win_bufs[0].reshape(depth, nb, N_HEADS_A, HEAD_DIM, w_buf), (0, 1, 4, 2, 3))
    s_win_v = jnp.transpose(win_bufs[1].reshape(depth, nb, N_HEADS_A, HEAD_DIM, w_buf), (0, 1, 4, 2, 3))
    p_win_k = jnp.transpose(jnp.stack(p_k).reshape(depth, 1, N_HEADS_A, HEAD_DIM, W_MAX), (0, 1, 4, 2, 3))
    p_win_v = jnp.transpose(jnp.stack(p_v).reshape(depth, 1, N_HEADS_A, HEAD_DIM, W_MAX), (0, 1, 4, 2, 3))
    return (y_prompt, y_sample, p_win_k, p_win_v, jnp.stack(p_pool), jnp.stack(p_conv),
            jnp.stack(p_h), s_win_k, s_win_v, jnp.stack(s_pool), jnp.stack(s_conv), jnp.stack(s_h))
```

```python
import functools

import numpy as np
import jax
import jax.numpy as jnp
from jax import lax
from jax.experimental import pallas as pl
from jax.experimental.pallas import tpu as pltpu

F32 = jnp.float32
BF16 = jnp.bfloat16

D_MODEL = 1024
HEAD_DIM = 64
D_A = 384
N_HEADS_A = D_A // HEAD_DIM
D_B = 256
D_C = 384
D_IN = 3 * D_A + D_B + 2 * D_C
D_FF = 2816
POOL_MAX = 16
CONV_W = 4
RG_C = 8.0
EPS = 1e-6
ROPE_THETA = 10000.0
W_MAX = 2048
PAST_LEN = 16384
DILATED_PATTERNS = ((128, 1), (512, 4), (2048, 16))
BAND = 128
MAX_DIL = 16
SUPER = BAND * MAX_DIL

LANES = 128
SUBLANES = 8
NEG = -1e30
VMEM_LIMIT = 56 * 1024 * 1024

PROJ_TILE = 1024
PROJ_SUB = 256
PROJ_AHEAD = 2
FFN_TILE = 512
HEAD_SPLIT = 3
D_A_BLK = D_A // HEAD_SPLIT

_Q0, _K0, _V0, _U0, _X0, _G0 = 0, D_A, 2 * D_A, 3 * D_A, 3 * D_A + D_B, 3 * D_A + D_B + D_C


def _rms(x, g):
    return x * lax.rsqrt(jnp.mean(x * x, axis=-1, keepdims=True) + EPS) * g


def _rope_tile(t, cos, sin_signed):
    lane = lax.broadcasted_iota(jnp.int32, t.shape, 1)
    first = (lane % HEAD_DIM) < (HEAD_DIM // 2)
    swapped = jnp.where(first, pltpu.roll(t, LANES - HEAD_DIM // 2, axis=1), pltpu.roll(t, HEAD_DIM // 2, axis=1))
    return t * cos + swapped * sin_signed


def _log_sigmoid(x):
    return jnp.minimum(x, 0.0) - jnp.log1p(jnp.exp(-jnp.abs(x)))


def _project(x, g1, w_in):
    hn = _rms(x, g1)
    return jnp.dot(hn.astype(BF16), w_in, preferred_element_type=F32)


def _pool_map(sums0, sums1, u, cnt0, cnt1, wpool, pscale):
    lo = lax.broadcasted_iota(jnp.int32, cnt0.shape, 1) < HEAD_DIM
    mean0 = jnp.where(lo, sums0[0], sums0[1]) / cnt0
    mean1 = jnp.where(lo, sums1[0], sums1[1]) / cnt1
    pooled = jnp.concatenate([mean0, mean1], axis=1) - u
    return jnp.dot(pooled.astype(BF16), wpool, preferred_element_type=F32) * pscale


def _window_sums_shifted(shift, narrow, wide):
    s = shift(0)
    for i in range(1, narrow):
        s = s + shift(i)
    s_narrow = s
    for i in range(narrow, wide):
        s = s + shift(i)
    return s_narrow, s


def _window_sums_rolled(xe, halo):
    s2 = xe + pltpu.roll(xe, 1, axis=0)
    s4 = s2 + pltpu.roll(s2, 2, axis=0)
    s8 = s4 + pltpu.roll(s4, 4, axis=0)
    s16 = s8 + pltpu.roll(s8, 8, axis=0)
    return s2[halo:], s4[halo:], s8[halo:], s16[halo:]


def _rglru_terms(xshift, convw, convb, wg, bg, lam):
    xc = convb + (xshift(3) * convw[0:1] + xshift(2) * convw[1:2] + xshift(1) * convw[2:3] + xshift(0) * convw[3:4])
    g = jnp.dot(xc.astype(BF16), wg, preferred_element_type=F32) + bg
    r = jax.nn.sigmoid(g[:, :D_C])
    ig = jax.nn.sigmoid(g[:, D_C:])
    half_log_a = (0.5 * RG_C) * r * _log_sigmoid(lam)
    a = jnp.exp(2.0 * half_log_a)
    b = (1.0 + a) * jnp.sqrt(-jnp.tanh(half_log_a)) * ig * xc
    return a, b


def _compose_step(a, b, shift):
    b = a * pltpu.roll(b, shift, axis=0) + b
    a = a * pltpu.roll(a, shift, axis=0)
    return a, b


def _scan_groups(a, b, h0_rows):
    row = lax.broadcasted_iota(jnp.int32, a.shape, 0) % SUBLANES
    for s in (1, 2, 4):
        keep = row >= s
        a_sh = jnp.where(keep, pltpu.roll(a, s, axis=0), 1.0)
        b_sh = jnp.where(keep, pltpu.roll(b, s, axis=0), 0.0)
        b = a * b_sh + b
        a = a * a_sh
    return a * h0_rows + b


def _scan_tile(a, b, h_last):
    n, c = a.shape
    a = jnp.concatenate([jnp.ones((SUBLANES, c), F32), a], axis=0)
    b = jnp.concatenate([jnp.zeros((SUBLANES, c), F32), b], axis=0)
    for s in (1, 2, 4):
        a, b = _compose_step(a, b, s)
    h = jnp.broadcast_to(h_last, (SUBLANES, c))
    out = []
    for j in range(1, n // SUBLANES + 1):
        rows = slice(j * SUBLANES, (j + 1) * SUBLANES)
        h = a[rows] * h + b[rows]
        out.append(h)
    return jnp.concatenate(out, axis=0)


def _prompt_proj_kernel(x_ref, g1_ref, win_ref, cr_ref, sr_ref, crs_ref, srs_ref, cb_ref, sb_ref,
                        wpool_ref, pscale_ref, convw_ref, convb_ref, wg_ref, bg_ref, lam_ref,
                        q_ref, k_ref, v_ref, obc_ref, pst_ref, cst_ref, hst_ref,
                        ubuf, xbuf, hcar):
    i = pl.program_id(0)
    tm = x_ref.shape[0]

    @pl.when(i == 0)
    def _():
        ubuf[0:POOL_MAX, :] = jnp.zeros((POOL_MAX, D_B), F32)
        xbuf[0:SUBLANES, :] = jnp.zeros((SUBLANES, D_C), F32)
        hcar[...] = jnp.zeros((SUBLANES, D_C), F32)

    def project(r0):
        rows = slice(r0, r0 + PROJ_SUB)
        proj = _project(x_ref[rows, :], g1_ref[...], win_ref[...])
        sub = i * (tm // PROJ_SUB) + r0 // PROJ_SUB
        cb = cb_ref[pl.ds(sub, 1), :]
        sb = sb_ref[pl.ds(sub, 1), :]
        cos = cb * cr_ref[...] - sb * sr_ref[...]
        sin = sb * crs_ref[...] + cb * srs_ref[...]
        for hp in range(D_A // LANES):
            sl = slice(hp * LANES, (hp + 1) * LANES)
            q_ref[hp, rows, :] = _rope_tile(proj[:, _Q0:_K0][:, sl], cos, sin)
            k_ref[hp, rows, :] = _rope_tile(proj[:, _K0:_V0][:, sl], cos, sin)
            v_ref[hp, rows, :] = proj[:, _V0:_U0][:, sl]
        u = proj[:, _U0:_X0]
        xr = proj[:, _X0:_G0]
        ubuf[POOL_MAX + r0:POOL_MAX + r0 + PROJ_SUB, :] = u
        xbuf[SUBLANES + r0:SUBLANES + r0 + PROJ_SUB, :] = xr
        return u, xr, proj[:, _G0:]

    def mix(r0, u, gt, h_last):
        rows = slice(r0, r0 + PROJ_SUB)
        pos1 = (i * tm + r0 + 1 + lax.broadcasted_iota(jnp.int32, (PROJ_SUB, LANES), 0)).astype(F32)
        lo = lax.broadcasted_iota(jnp.int32, (PROJ_SUB, LANES), 1) < HEAD_DIM
        cnt0 = jnp.minimum(pos1, jnp.where(lo, 2.0, 4.0))
        cnt1 = jnp.minimum(pos1, jnp.where(lo, 8.0, 16.0))
        ext = slice(r0, r0 + PROJ_SUB + POOL_MAX)
        s2, s4, _, _ = _window_sums_rolled(ubuf[ext, 0:LANES], POOL_MAX)
        _, _, s8, s16 = _window_sums_rolled(ubuf[ext, LANES:D_B], POOL_MAX)
        o_b = _pool_map((s2, s4), (s8, s16), u, cnt0, cnt1, wpool_ref[...], pscale_ref[...])

        a, b = _rglru_terms(lambda s: xbuf[pl.ds(SUBLANES + r0 - s, PROJ_SUB), :], convw_ref[...],
                            convb_ref[...], wg_ref[...], bg_ref[...], lam_ref[...])
        h = _scan_tile(a, b, h_last)
        obc_ref[rows, 0:D_B] = o_b
        obc_ref[rows, D_B:] = h * jax.nn.gelu(gt)
        return h

    h_last = hcar[SUBLANES - 1:SUBLANES, :]
    starts = list(range(0, tm, PROJ_SUB))
    pending = [project(r0) for r0 in starts[:PROJ_AHEAD]]
    for k, r0 in enumerate(starts):
        u, xr, gt = pending.pop(0)
        if k + PROJ_AHEAD < len(starts):
            pending.append(project(starts[k + PROJ_AHEAD]))
        h = mix(r0, u, gt, h_last)
        h_last = h[PROJ_SUB - 1:, :]

    ubuf[0:POOL_MAX, :] = u[PROJ_SUB - POOL_MAX:, :]
    xbuf[0:SUBLANES, :] = xr[PROJ_SUB - SUBLANES:, :]
    hcar[...] = h[PROJ_SUB - SUBLANES:, :]
    pst_ref[...] = u[PROJ_SUB - POOL_MAX:, :]
    cst_ref[...] = xr[PROJ_SUB - SUBLANES:, :]
    hst_ref[...] = h[PROJ_SUB - SUBLANES:, :]


def _const_spec(shape):
    return pl.BlockSpec(shape, lambda *_: (0,) * len(shape))


def _layer_spec(layer, shape):
    return pl.BlockSpec((None,) + tuple(shape), lambda *_: (layer,) + (0,) * len(shape),
                        pipeline_mode=pl.Buffered(1))


def _prompt_proj(x, layer, pw, rope, tm):
    t = x.shape[0]
    n_hp = D_A // LANES
    qkv_shape = jax.ShapeDtypeStruct((n_hp, t, LANES), F32)
    qkv_spec = pl.BlockSpec((n_hp, tm, LANES), lambda i: (0, i, 0))
    return pl.pallas_call(
        _prompt_proj_kernel,
        grid=(t // tm,),
        in_specs=([pl.BlockSpec((tm, D_MODEL), lambda i: (i, 0))]
                  + [_layer_spec(layer, pw[k].shape[1:]) for k in _PROJ_PARAMS]
                  + [_const_spec(r.shape) for r in rope]
                  + [_layer_spec(layer, pw[k].shape[1:]) for k in _MIX_PARAMS]),
        out_specs=[
            qkv_spec, qkv_spec, qkv_spec,
            pl.BlockSpec((tm, D_B + D_C), lambda i: (i, 0)),
            _const_spec((POOL_MAX, D_B)),
            _const_spec((SUBLANES, D_C)),
            _const_spec((SUBLANES, D_C)),
        ],
        out_shape=[
            qkv_shape, qkv_shape, qkv_shape,
            jax.ShapeDtypeStruct((t, D_B + D_C), F32),
            jax.ShapeDtypeStruct((POOL_MAX, D_B), F32),
            jax.ShapeDtypeStruct((SUBLANES, D_C), F32),
            jax.ShapeDtypeStruct((SUBLANES, D_C), F32),
        ],
        scratch_shapes=[
            pltpu.VMEM((POOL_MAX + tm, D_B), F32),
            pltpu.VMEM((SUBLANES + tm, D_C), F32),
            pltpu.VMEM((SUBLANES, D_C), F32),
        ],
        compiler_params=pltpu.CompilerParams(dimension_semantics=("arbitrary",), vmem_limit_bytes=VMEM_LIMIT),
        name="prompt_proj",
    )(x, *[pw[k] for k in _PROJ_PARAMS], *rope, *[pw[k] for k in _MIX_PARAMS])


def _band_bias(dil):
    slabs = MAX_DIL // dil
    rows = BAND // slabs
    row = lax.broadcasted_iota(jnp.int32, (BAND, 2 * BAND), 0)
    col = lax.broadcasted_iota(jnp.int32, (BAND, 2 * BAND), 1)
    qi = slabs * (row % rows) + row // rows
    ki = slabs * (col % (2 * rows)) + col // (2 * rows)
    valid = (ki >= qi) & (ki <= qi + BAND)
    bias = jnp.where(valid, 0.0, NEG)
    bias_first = jnp.where(valid & (ki >= BAND), 0.0, NEG)
    return bias, bias_first


def _prompt_attn_kernel(q_ref, k_ref, v_ref, o_ref, kt_ref, vt_ref, q16, kcat, vcat, acc_s, m_s, l_s):
    i = pl.program_id(1)

    @pl.when(i == pl.num_programs(1) - 1)
    def _():
        kt_ref[...] = k_ref[...].T
        vt_ref[...] = v_ref[...].T

    @pl.when(i == 0)
    def _():
        kcat[0:BAND, :] = jnp.zeros((BAND, MAX_DIL * LANES), F32)
        vcat[0:BAND, :] = jnp.zeros((BAND, MAX_DIL * LANES), F32)

    @pl.when(i > 0)
    def _():
        kcat[0:BAND, :] = kcat[BAND:, :]
        vcat[0:BAND, :] = vcat[BAND:, :]

    for r in range(MAX_DIL):
        ls = slice(r * LANES, (r + 1) * LANES)
        q16[:, ls] = q_ref[pl.ds(r, BAND, stride=MAX_DIL), :]
        kcat[BAND:, ls] = k_ref[pl.ds(r, BAND, stride=MAX_DIL), :]
        vcat[BAND:, ls] = v_ref[pl.ds(r, BAND, stride=MAX_DIL), :]

    head0 = lax.broadcasted_iota(jnp.int32, (BAND, LANES), 1) < HEAD_DIM
    ones_cols = jnp.ones((2 * BAND, LANES), BF16)

    units = []
    for stage, (_, dil) in enumerate(DILATED_PATTERNS):
        slabs = MAX_DIL // dil
        rows = BAND // slabs
        bias, bias_first = _band_bias(dil)
        bias_blk0 = jnp.where(i == 0, bias_first, bias)
        for blk in range(slabs):
            for res in range(dil):
                units.append((stage, slabs, rows, blk, [res + dil * c for c in range(slabs)],
                              bias_blk0 if blk == 0 else bias))

    def scores(unit):
        _, _, rows, blk, groups, _ = unit
        q_rows = slice(rows * blk, rows * (blk + 1))
        k_rows = slice(BAND + rows * (blk - 1), BAND + rows * (blk + 1))
        lanes = [slice(g * LANES, (g + 1) * LANES) for g in groups]
        qb = jnp.concatenate([q16[q_rows, ls] for ls in lanes], axis=0)
        kk = jnp.concatenate([kcat[k_rows, ls] for ls in lanes], axis=0).astype(BF16)
        vv = jnp.concatenate([vcat[k_rows, ls] for ls in lanes], axis=0).astype(BF16)
        q2 = jnp.concatenate([jnp.where(head0, qb, 0.0), jnp.where(head0, 0.0, qb)], axis=0).astype(BF16)
        s2 = lax.dot_general(q2, kk, (((1,), (1,)), ((), ())), preferred_element_type=F32)
        return s2, jnp.concatenate([vv, ones_cols], axis=1)

    def finish(unit, s2, vv1):
        stage, _, rows, blk, groups, b_u = unit
        q_rows = slice(rows * blk, rows * (blk + 1))
        lanes = [slice(g * LANES, (g + 1) * LANES) for g in groups]
        ms, ps = [], []
        for hh in range(2):
            s = s2[hh * BAND:(hh + 1) * BAND] + b_u
            m = jnp.max(s, axis=1, keepdims=True)
            ms.append(m)
            ps.append(jnp.exp((s - m).astype(BF16)))
        pv2 = jnp.dot(jnp.concatenate(ps, axis=0), vv1, preferred_element_type=F32)
        parts = [(ms[hh], pv2[hh * BAND:(hh + 1) * BAND, LANES:], pv2[hh * BAND:(hh + 1) * BAND, :LANES])
                 for hh in range(2)]
        m_b = jnp.where(head0, parts[0][0], parts[1][0])
        l_b = jnp.where(head0, parts[0][1], parts[1][1])
        pv_b = jnp.where(head0, parts[0][2], parts[1][2])
        if stage > 0:
            m_o = jnp.concatenate([m_s[q_rows, ls] for ls in lanes], axis=0)
            l_o = jnp.concatenate([l_s[q_rows, ls] for ls in lanes], axis=0)
            a_o = jnp.concatenate([acc_s[q_rows, ls] for ls in lanes], axis=0)
            m_n = jnp.maximum(m_o, m_b)
            w_o = jnp.exp(m_o - m_n)
            w_b = jnp.exp(m_b - m_n)
            l_b = l_o * w_o + l_b * w_b
            pv_b = a_o * w_o + pv_b * w_b
            m_b = m_n
        last = stage == len(DILATED_PATTERNS) - 1
        if last:
            out = pv_b / l_b
        for c, ls in enumerate(lanes):
            sub = slice(c * rows, (c + 1) * rows)
            if last:
                o_ref[pl.ds(MAX_DIL * rows * blk + groups[c], rows, stride=MAX_DIL), :] = out[sub]
            else:
                m_s[q_rows, ls] = m_b[sub]
                l_s[q_rows, ls] = l_b[sub]
                acc_s[q_rows, ls] = pv_b[sub]

    for unit in units:
        finish(unit, *scores(unit))


def _prompt_attn(q, k, v):
    n_hp, t, _ = q.shape
    n_super = t // SUPER
    width = MAX_DIL * LANES
    cur = pl.BlockSpec((None, SUPER, LANES), lambda h, i: (h, i, 0))
    win = pl.BlockSpec((None, LANES, SUPER), lambda h, i: (h, 0, 0))
    win_shape = jax.ShapeDtypeStruct((n_hp, LANES, SUPER), F32)
    return pl.pallas_call(
        _prompt_attn_kernel,
        grid=(n_hp, n_super),
        in_specs=[cur, cur, cur],
        out_specs=[cur, win, win],
        out_shape=[jax.ShapeDtypeStruct(q.shape, F32), win_shape, win_shape],
        scratch_shapes=[
            pltpu.VMEM((BAND, width), F32),
            pltpu.VMEM((2 * BAND, width), F32),
            pltpu.VMEM((2 * BAND, width), F32),
            pltpu.VMEM((BAND, width), F32),
            pltpu.VMEM((BAND, width), F32),
            pltpu.VMEM((BAND, width), F32),
        ],
        compiler_params=pltpu.CompilerParams(dimension_semantics=("arbitrary", "arbitrary"),
                                             vmem_limit_bytes=VMEM_LIMIT),
        name="prompt_attn",
    )(q, k, v)


FF_CHUNK = 256


def _outffn_begin(x_ref, oa_ref, obc_ref, wout_ref, g2_ref):
    mix = jnp.concatenate([oa_ref[c] for c in range(oa_ref.shape[0])] + [obc_ref[...]], axis=1).astype(BF16)
    x1 = x_ref[...] + jnp.dot(mix, wout_ref[...], preferred_element_type=F32)
    return x1, _rms(x1, g2_ref[...]).astype(BF16)


def _ffn_chunks(h2, acc, wgu_ref, wdown_ref, chunks):
    for c in chunks:
        g = jnp.dot(h2, wgu_ref[:, c * FF_CHUNK:(c + 1) * FF_CHUNK], preferred_element_type=F32)
        up = jnp.dot(h2, wgu_ref[:, D_FF + c * FF_CHUNK:D_FF + (c + 1) * FF_CHUNK], preferred_element_type=F32)
        act = (jax.nn.silu(g) * up).astype(BF16)
        acc = acc + jnp.dot(act, wdown_ref[c * FF_CHUNK:(c + 1) * FF_CHUNK, :], preferred_element_type=F32)
    return acc


def _outffn_rows(x_ref, oa_ref, obc_ref, wout_ref, g2_ref, wgu_ref, wdown_ref, fg_ref, final):
    x1, h2 = _outffn_begin(x_ref, oa_ref, obc_ref, wout_ref, g2_ref)
    acc = _ffn_chunks(h2, x1, wgu_ref, wdown_ref, range(D_FF // FF_CHUNK))
    return _rms(acc, fg_ref[...]) if final else acc


def _outffn_kernel(x_ref, oa_ref, obc_ref, wout_ref, g2_ref, wgu_ref, wdown_ref, fg_ref, o_ref, *, final):
    o_ref[...] = _outffn_rows(x_ref, oa_ref, obc_ref, wout_ref, g2_ref, wgu_ref, wdown_ref, fg_ref, final)


def _outffn_specs(oa, layer, pw, tm, index):
    return [
        pl.BlockSpec((tm, D_MODEL), lambda *g: (index(*g), 0)),
        pl.BlockSpec((oa.shape[0], tm, oa.shape[2]), lambda *g: (0, index(*g), 0)),
        pl.BlockSpec((tm, D_B + D_C), lambda *g: (index(*g), 0)),
    ] + [_layer_spec(layer, pw[k].shape[1:]) for k in _FFN_PARAMS] + [_const_spec((1, D_MODEL))]


def _outffn(x, oa, obc, layer, pw, final_g, final, tm):
    t = x.shape[0]
    return pl.pallas_call(
        functools.partial(_outffn_kernel, final=final),
        grid=(t // tm,),
        in_specs=_outffn_specs(oa, layer, pw, tm, lambda i: i),
        out_specs=pl.BlockSpec((tm, D_MODEL), lambda i: (i, 0)),
        out_shape=jax.ShapeDtypeStruct((t, D_MODEL), F32),
        compiler_params=pltpu.CompilerParams(dimension_semantics=("parallel",), vmem_limit_bytes=VMEM_LIMIT),
        name="outffn",
    )(x, oa, obc, *[pw[k] for k in _FFN_PARAMS], final_g)


def _sample_proj_kernel(x_ref, g1_ref, win_ref, cos_ref, sin_ref, wpool_ref, pscale_ref, convw_ref, convb_ref,
                        wg_ref, bg_ref, lam_ref, spool_ref, sconv_ref, h0_ref,
                        q_ref, k_ref, v_ref, obc_ref, npool_ref, nconv_ref, nh_ref,
                        ubuf, xbuf):
    n = x_ref.shape[0]
    nb = n // SUBLANES
    proj = _project(x_ref[...], g1_ref[...], win_ref[...])
    cos = cos_ref[...]
    sin = sin_ref[...]
    for c in range(HEAD_SPLIT):
        sl = slice(c * D_A_BLK, (c + 1) * D_A_BLK)
        q_ref[c] = _rope_tile(proj[:, _Q0:_K0][:, sl], cos, sin)
        k_ref[c] = _rope_tile(proj[:, _K0:_V0][:, sl], cos, sin)
        v_ref[c] = proj[:, _V0:_U0][:, sl]

    u = proj[:, _U0:_X0]
    xr = proj[:, _X0:_G0]
    gt = proj[:, _G0:]
    ubuf[:, 0:POOL_MAX, :] = spool_ref[...]
    ubuf[:, POOL_MAX:, :] = u.reshape(nb, SUBLANES, D_B)
    xbuf[:, 0:SUBLANES, :] = sconv_ref[...]
    xbuf[:, SUBLANES:, :] = xr.reshape(nb, SUBLANES, D_C)

    lo = lax.broadcasted_iota(jnp.int32, (n, LANES), 1) < HEAD_DIM
    cnt0 = jnp.where(lo, 2.0, 4.0)
    cnt1 = jnp.where(lo, 8.0, 16.0)
    sums0 = _window_sums_shifted(lambda s: ubuf[:, pl.ds(POOL_MAX - s, SUBLANES), 0:LANES].reshape(n, LANES), 2, 4)
    sums1 = _window_sums_shifted(lambda s: ubuf[:, pl.ds(POOL_MAX - s, SUBLANES), LANES:D_B].reshape(n, LANES), 8, 16)
    o_b = _pool_map(sums0, sums1, u, cnt0, cnt1, wpool_ref[...], pscale_ref[...])

    a, b = _rglru_terms(lambda s: xbuf[:, pl.ds(SUBLANES - s, SUBLANES), :].reshape(n, D_C), convw_ref[...],
                        convb_ref[...], wg_ref[...], bg_ref[...], lam_ref[...])
    h = _scan_groups(a, b, h0_ref[...])
    obc_ref[:, 0:D_B] = o_b
    obc_ref[:, D_B:] = h * jax.nn.gelu(gt)
    npool_ref[...] = ubuf[:, SUBLANES:, :]
    nconv_ref[...] = xr.reshape(nb, SUBLANES, D_C)
    nh_ref[...] = h


def _sample_proj(x, layer, pw, cos, sin, spool, sconv, h0rows):
    n = x.shape[0]
    nb = n // SUBLANES
    shapes = [
        jax.ShapeDtypeStruct((HEAD_SPLIT, n, D_A_BLK), F32), jax.ShapeDtypeStruct((HEAD_SPLIT, n, D_A_BLK), F32),
        jax.ShapeDtypeStruct((HEAD_SPLIT, n, D_A_BLK), F32),
        jax.ShapeDtypeStruct((n, D_B + D_C), F32),
        jax.ShapeDtypeStruct((nb, POOL_MAX, D_B), F32),
        jax.ShapeDtypeStruct((nb, SUBLANES, D_C), F32),
        jax.ShapeDtypeStruct((n, D_C), F32),
    ]
    data = (spool, sconv, h0rows)
    in_specs = ([_const_spec(x.shape)] + [_layer_spec(layer, pw[k].shape[1:]) for k in _PROJ_PARAMS]
                + [_const_spec(cos.shape), _const_spec(sin.shape)]
                + [_layer_spec(layer, pw[k].shape[1:]) for k in _MIX_PARAMS] + [_const_spec(a.shape) for a in data])
    args = (x, *[pw[k] for k in _PROJ_PARAMS], cos, sin, *[pw[k] for k in _MIX_PARAMS], *data)
    return pl.pallas_call(
        _sample_proj_kernel,
        grid=(1,),
        in_specs=in_specs,
        out_specs=[_const_spec(s.shape) for s in shapes],
        out_shape=shapes,
        scratch_shapes=[
            pltpu.VMEM((nb, POOL_MAX + SUBLANES, D_B), F32),
            pltpu.VMEM((nb, 2 * SUBLANES, D_C), F32),
        ],
        compiler_params=pltpu.CompilerParams(dimension_semantics=("arbitrary",), vmem_limit_bytes=VMEM_LIMIT),
        name="sample_proj",
    )(*args)


def _sample_key_counts(w_buf, t_new, n_heads):
    k = np.arange(w_buf + LANES)[None, :]
    dist = w_buf + np.arange(t_new)[:, None] - k
    cnt = np.zeros(dist.shape, np.float32)
    for window, dil in DILATED_PATTERNS:
        cnt += ((dist >= 0) & (dist <= window) & (dist % dil == 0)).astype(np.float32)
    return np.tile(cnt, (n_heads, 1))


def _shift_window(c_ref, new_ref, s_ref):
    da, w_buf = c_ref.shape
    t_new = new_ref.shape[0]
    tail = lax.broadcasted_iota(jnp.int32, (da, LANES), 1) >= LANES - t_new
    rolled = pltpu.roll(c_ref[...], w_buf - t_new, axis=1)
    new_t = jnp.concatenate([jnp.zeros((LANES - t_new, da), F32), new_ref[...]], axis=0).T
    s_ref[:, 0:w_buf - LANES] = rolled[:, 0:w_buf - LANES]
    s_ref[:, w_buf - LANES:] = jnp.where(tail, new_t, rolled[:, w_buf - LANES:])


def _sample_attn_scores(ck_ref, q_ref, kn_ref):
    da = ck_ref.shape[0]
    t_new = q_ref.shape[0]
    qt = jnp.concatenate([q_ref[...]] * (da // HEAD_DIM), axis=0)
    row = lax.broadcasted_iota(jnp.int32, qt.shape, 0)
    lane = lax.broadcasted_iota(jnp.int32, qt.shape, 1)
    own = (row // t_new) == (lane // HEAD_DIM)
    qbd = jnp.where(own, qt, 0.0).astype(BF16)
    kn_pad = jnp.concatenate([kn_ref[...], jnp.zeros((LANES - t_new, da), F32)], axis=0).astype(BF16)
    return jnp.concatenate([
        jnp.dot(qbd, ck_ref[...].astype(BF16), preferred_element_type=F32),
        lax.dot_general(qbd, kn_pad, (((1,), (1,)), ((), ())), preferred_element_type=F32)], axis=1)


def _sample_attn_finish(s, cnt_ref, cv_ref, vn_ref, oa_ref):
    da, w_buf = cv_ref.shape
    t_new = vn_ref.shape[0]
    cv = cv_ref[...]
    vn = vn_ref[...]
    zpad = jnp.zeros((LANES - t_new, da), F32)

    cnt = cnt_ref[...]
    s = jnp.where(cnt > 0.0, s, NEG)
    m = jnp.max(s, axis=1, keepdims=True)
    p = jnp.exp(s - m) * cnt
    den = jnp.sum(p, axis=1, keepdims=True)
    pb = p.astype(BF16)
    vn_pad = jnp.concatenate([vn, zpad], axis=0).astype(BF16)
    o_n = (lax.dot_general(pb[:, :w_buf], cv.astype(BF16), (((1,), (1,)), ((), ())), preferred_element_type=F32)
           + jnp.dot(pb[:, w_buf:], vn_pad, preferred_element_type=F32)) / den
    out = jnp.zeros((t_new, da), F32)
    lane_o = lax.broadcasted_iota(jnp.int32, (t_new, da), 1) // HEAD_DIM
    for h in range(da // HEAD_DIM):
        out = jnp.where(lane_o == h, o_n[h * t_new:(h + 1) * t_new, :], out)
    oa_ref[...] = out


def _outffn_sample_attn_kernel(x_ref, oa_ref, obc_ref, wout_ref, g2_ref, wgu_ref, wdown_ref, fg_ref,
                               cnt_ref, ck_ref, cv_ref, q_ref, kn_ref, vn_ref, *rest, final, parts):
    o_ref, sk_ref, sv_ref, oas_ref, acc_scr, h2_scr = rest[-6:]

    for part, chunks in enumerate(parts):
        @pl.when(pl.program_id(1) == part)
        def _(part=part, chunks=chunks):
            scores = _sample_attn_scores(ck_ref, q_ref, kn_ref)
            _shift_window(ck_ref, kn_ref, sk_ref)
            _shift_window(cv_ref, vn_ref, sv_ref)
            if part == 0:
                acc, h2 = _outffn_begin(x_ref, oa_ref, obc_ref, wout_ref, g2_ref)
                h2_scr[...] = h2
            else:
                acc, h2 = acc_scr[...], h2_scr[...]
            acc = _ffn_chunks(h2, acc, wgu_ref, wdown_ref, chunks)
            if part == len(parts) - 1:
                o_ref[...] = _rms(acc, fg_ref[...]) if final else acc
            else:
                acc_scr[...] = acc
            _sample_attn_finish(scores, cnt_ref, cv_ref, vn_ref, oas_ref)


def _outffn_sample_attn(x, oa, obc, layer, pw, final_g, final, tm, cache_k, cache_v, q, kn, vn, win_bufs, counts):
    t = x.shape[0]
    n_tiles = t // tm
    depth, nb, split, da, w_buf = cache_k.shape
    t_new = q.shape[1] // nb
    per_tile = nb * split // n_tiles
    assert per_tile * n_tiles == nb * split
    parts = [tuple(int(c) for c in p) for p in np.array_split(np.arange(D_FF // FF_CHUNK), per_tile)[::-1]]

    def group(i, j):
        g = i * per_tile + j
        return g // split, g % split

    def cache_index(i, j):
        b, c = group(i, j)
        return (layer, b, c, 0, 0)

    def tok_index(i, j):
        b, c = group(i, j)
        return (c, b, 0)

    cache_spec = pl.BlockSpec((None, None, None, da, w_buf), cache_index)
    tok_spec = pl.BlockSpec((None, t_new, da), tok_index)
    in_specs = _outffn_specs(oa, layer, pw, tm, lambda i, j: i) + [
        _const_spec(counts.shape), cache_spec, cache_spec, tok_spec, tok_spec, tok_spec]
    args = [x, oa, obc, *[pw[k] for k in _FFN_PARAMS], final_g, counts, cache_k, cache_v, q, kn, vn]
    aliases = {}
    if win_bufs is not None:
        in_specs += [pl.BlockSpec(memory_space=pl.ANY)] * 2
        aliases = {len(args): 1, len(args) + 1: 2}
        args += list(win_bufs)
    win_shape = jax.ShapeDtypeStruct(cache_k.shape, F32)
    y, sk, sv, oa_s = pl.pallas_call(
        functools.partial(_outffn_sample_attn_kernel, final=final, parts=parts),
        grid=(n_tiles, per_tile),
        in_specs=in_specs,
        out_specs=[pl.BlockSpec((tm, D_MODEL), lambda i, j: (i, 0)), cache_spec, cache_spec, tok_spec],
        out_shape=[jax.ShapeDtypeStruct((t, D_MODEL), F32), win_shape, win_shape, jax.ShapeDtypeStruct(q.shape, F32)],
        input_output_aliases=aliases,
        scratch_shapes=[pltpu.VMEM((tm, D_MODEL), F32), pltpu.VMEM((tm, D_MODEL), BF16)],
        compiler_params=pltpu.CompilerParams(dimension_semantics=("arbitrary", "arbitrary"),
                                             vmem_limit_bytes=VMEM_LIMIT),
        name="outffn_sample_attn",
    )(*args)
    return y, (sk, sv), oa_s


def _rope_tables(pos):
    half = HEAD_DIM // 2
    inv = jnp.power(ROPE_THETA, -2.0 * jnp.arange(half, dtype=F32) / HEAD_DIM)
    ang = pos.astype(F32)[:, None] * inv[None, :]
    cos = jnp.tile(jnp.cos(ang), (1, LANES // half))
    sin = jnp.sin(ang)
    sin_signed = jnp.tile(jnp.concatenate([-sin, sin], axis=1), (1, LANES // HEAD_DIM))
    return cos, sin_signed


def _rope_tile_tables(seq):
    half = HEAD_DIM // 2
    inv = jnp.power(ROPE_THETA, -2.0 * jnp.arange(half, dtype=F32) / HEAD_DIM)
    inv = jnp.tile(inv, LANES // half)[None, :]
    sign = jnp.tile(jnp.concatenate([-jnp.ones(half, F32), jnp.ones(half, F32)]), LANES // HEAD_DIM)[None, :]
    ang_r = jnp.arange(PROJ_SUB, dtype=F32)[:, None] * inv
    ang_b = (jnp.arange(seq // PROJ_SUB, dtype=F32) * PROJ_SUB)[:, None] * inv
    cr, sr = jnp.cos(ang_r), jnp.sin(ang_r)
    return cr, sr, sign * cr, sign * sr, jnp.cos(ang_b), jnp.sin(ang_b)


def _block_diag(w):
    depth, n, c, d = w.shape
    eye = jnp.eye(n, dtype=w.dtype)
    return (eye[None, :, None, :, None] * w[:, :, :, None, :]).reshape(depth, n * c, n * d)


_PROJ_PARAMS = ("g1", "w_in")
_MIX_PARAMS = ("wpool", "pscale", "convw", "convb", "wg", "bg", "lam")
_FFN_PARAMS = ("w_out", "g2", "w_gu", "w_down")


def _stack_params(norm1_g, w_in, pool_w, pool_scale, conv_w, conv_b, gate_a_w, gate_a_b, gate_x_w, gate_x_b, lru_lambda,
                  w_out, norm2_g, w_gu, w_down):
    q_scale = jnp.where(jnp.arange(D_IN) < D_A, HEAD_DIM ** -0.5, 1.0).astype(F32)
    return dict(
        g1=norm1_g[:, None, :],
        w_in=(w_in * q_scale).astype(BF16),
        wpool=_block_diag(pool_w).astype(BF16),
        pscale=pool_scale[:, None, :],
        convw=conv_w,
        convb=conv_b[:, None, :],
        wg=jnp.concatenate([_block_diag(gate_a_w), _block_diag(gate_x_w)], axis=2).astype(BF16),
        bg=jnp.concatenate([gate_a_b, gate_x_b], axis=1)[:, None, :],
        lam=lru_lambda[:, None, :],
        w_out=w_out.astype(BF16),
        g2=norm2_g[:, None, :],
        w_gu=w_gu.astype(BF16),
        w_down=w_down.astype(BF16),
    )


def kernel(x_prompt, x_sample, cache_win_k, cache_win_v, state_pool, state_conv, state_rglru, norm1_g, w_in, pool_w,
           pool_scale, conv_w, conv_b, gate_a_w, gate_a_b, gate_x_w, gate_x_b, lru_lambda, w_out, norm2_g, w_gu,
           w_down, final_g):
    batch, seq, _ = x_prompt.shape
    nb, t_new, _ = x_sample.shape
    depth, _, w_buf = cache_win_k.shape[:3]
    assert batch == 1 and seq % SUPER == 0 and seq % PROJ_TILE == 0 and seq % FFN_TILE == 0
    assert t_new == SUBLANES and w_buf == W_MAX and SUPER == W_MAX

    rope_p = _rope_tile_tables(seq)
    cos_s, sin_s = _rope_tables(PAST_LEN + jnp.arange(t_new, dtype=jnp.int32))
    cos_s = jnp.tile(cos_s, (nb, 1))
    sin_s = jnp.tile(sin_s, (nb, 1))
    counts = jnp.asarray(_sample_key_counts(w_buf, t_new, D_A_BLK // HEAD_DIM))
    fg = final_g[None, :]

    ck = jnp.transpose(cache_win_k, (0, 1, 3, 4, 2)).reshape(depth, nb, HEAD_SPLIT, D_A_BLK, w_buf)
    cv = jnp.transpose(cache_win_v, (0, 1, 3, 4, 2)).reshape(depth, nb, HEAD_SPLIT, D_A_BLK, w_buf)
    spool = jnp.pad(state_pool, ((0, 0), (0, 0), (1, 0), (0, 0)))
    sconv = jnp.pad(state_conv, ((0, 0), (0, 0), (SUBLANES - (CONV_W - 1), 0), (0, 0)))
    h0rows = jnp.repeat(state_rglru, t_new, axis=1)

    hp = x_prompt.reshape(seq, D_MODEL)
    hs = x_sample.reshape(nb * t_new, D_MODEL)
    pw = _stack_params(norm1_g, w_in, pool_w, pool_scale, conv_w, conv_b, gate_a_w, gate_a_b, gate_x_w, gate_x_b,
                       lru_lambda, w_out, norm2_g, w_gu, w_down)
    win_bufs = None
    p_k, p_v, p_pool, p_conv, p_h, s_pool, s_conv, s_h = [], [], [], [], [], [], [], []
    for l in range(depth):
        last = l == depth - 1
        qs, kn, vn, obc_s, npool, nconv, nh = _sample_proj(hs, l, pw, cos_s, sin_s, spool[l], sconv[l], h0rows[l])
        q, k, v, obc, pst, cst, hst = _prompt_proj(hp, l, pw, rope_p, PROJ_TILE)
        oa, kt, vt = _prompt_attn(q, k, v)
        hp, win_bufs, oa_s = _outffn_sample_attn(hp, oa, obc, l, pw, fg, last, FFN_TILE, ck, cv, qs, kn, vn,
                                                 win_bufs, counts)
        p_k.append(kt)
        p_v.append(vt)
        p_pool.append(pst[None, 1:, :])
        p_conv.append(cst[None, SUBLANES - (CONV_W - 1):, :])
        p_h.append(hst[SUBLANES - 1:, :])
        hs = _outffn(hs, oa_s, obc_s, l, pw, fg, last, nb * t_new)
        s_pool.append(npool[:, 1:, :])
        s_conv.append(nconv[:, SUBLANES - (CONV_W - 1):, :])
        s_h.append(nh.reshape(nb, t_new, D_C)[:, t_new - 1, :])

    y_prompt = hp.reshape(batch, seq, D_MODEL)
    y_sample = hs.reshape(nb, t_new, D_MODEL)
    s_win_k = jnp.transpose(win_bufs[0].reshape(depth, nb, N_HEADS_A, HEAD_DIM, w_buf), (0, 1, 4, 2, 3))
    s_win_v = jnp.transpose(win_bufs[1].reshape(depth, nb, N_HEADS_A, HEAD_DIM, w_buf), (0, 1, 4, 2, 3))
    p_win_k = jnp.transpose(jnp.stack(p_k).reshape(depth, 1, N_HEADS_A, HEAD_DIM, W_MAX), (0, 1, 4, 2, 3))
    p_win_v = jnp.transpose(jnp.stack(p_v).reshape(depth, 1, N_HEADS_A, HEAD_DIM, W_MAX), (0, 1, 4, 2, 3))
    return (y_prompt, y_sample, p_win_k, p_win_v, jnp.stack(p_pool), jnp.stack(p_conv),
            jnp.stack(p_h), s_win_k, s_win_v, jnp.stack(s_pool), jnp.stack(s_conv), jnp.stack(s_h))
```

```python
import functools

import numpy as np
import jax
import jax.numpy as jnp
from jax import lax
from jax.experimental import pallas as pl
from jax.experimental.pallas import tpu as pltpu

F32 = jnp.float32
BF16 = jnp.bfloat16

D_MODEL = 1024
HEAD_DIM = 64
D_A = 384
N_HEADS_A = D_A // HEAD_DIM
D_B = 256
D_C = 384
D_IN = 3 * D_A + D_B + 2 * D_C
D_FF = 2816
POOL_MAX = 16
CONV_W = 4
RG_C = 8.0
EPS = 1e-6
ROPE_THETA = 10000.0
W_MAX = 2048
PAST_LEN = 16384
DILATED_PATTERNS = ((128, 1), (512, 4), (2048, 16))
BAND = 128
MAX_DIL = 16
SUPER = BAND * MAX_DIL

LANES = 128
SUBLANES = 8
NEG = -1e30
VMEM_LIMIT = 56 * 1024 * 1024

PROJ_TILE = 1024
PROJ_SUB = 256
PROJ_AHEAD = 2
FFN_TILE = 512
HEAD_SPLIT = 3
D_A_BLK = D_A // HEAD_SPLIT

_Q0, _K0, _V0, _U0, _X0, _G0 = 0, D_A, 2 * D_A, 3 * D_A, 3 * D_A + D_B, 3 * D_A + D_B + D_C


def _rms(x, g):
    return x * lax.rsqrt(jnp.mean(x * x, axis=-1, keepdims=True) + EPS) * g


def _rope_tile(t, cos, sin_signed):
    lane = lax.broadcasted_iota(jnp.int32, t.shape, 1)
    first = (lane % HEAD_DIM) < (HEAD_DIM // 2)
    swapped = jnp.where(first, pltpu.roll(t, LANES - HEAD_DIM // 2, axis=1), pltpu.roll(t, HEAD_DIM // 2, axis=1))
    return t * cos + swapped * sin_signed


def _log_sigmoid(x):
    return jnp.minimum(x, 0.0) - jnp.log1p(jnp.exp(-jnp.abs(x)))


def _project(x, g1, w_in):
    hn = _rms(x, g1)
    return jnp.dot(hn.astype(BF16), w_in, preferred_element_type=F32)


def _pool_map(sums0, sums1, u, cnt0, cnt1, wpool, pscale):
    lo = lax.broadcasted_iota(jnp.int32, cnt0.shape, 1) < HEAD_DIM
    mean0 = jnp.where(lo, sums0[0], sums0[1]) / cnt0
    mean1 = jnp.where(lo, sums1[0], sums1[1]) / cnt1
    pooled = jnp.concatenate([mean0, mean1], axis=1) - u
    return jnp.dot(pooled.astype(BF16), wpool, preferred_element_type=F32) * pscale


def _window_sums_shifted(shift, narrow, wide):
    s = shift(0)
    for i in range(1, narrow):
        s = s + shift(i)
    s_narrow = s
    for i in range(narrow, wide):
        s = s + shift(i)
    return s_narrow, s


def _window_sums_rolled(xe, halo):
    s2 = xe + pltpu.roll(xe, 1, axis=0)
    s4 = s2 + pltpu.roll(s2, 2, axis=0)
    s8 = s4 + pltpu.roll(s4, 4, axis=0)
    s16 = s8 + pltpu.roll(s8, 8, axis=0)
    return s2[halo:], s4[halo:], s8[halo:], s16[halo:]


def _rglru_terms(xshift, convw, convb, wg, bg, lam):
    xc = convb + (xshift(3) * convw[0:1] + xshift(2) * convw[1:2] + xshift(1) * convw[2:3] + xshift(0) * convw[3:4])
    g = jnp.dot(xc.astype(BF16), wg, preferred_element_type=F32) + bg
    r = jax.nn.sigmoid(g[:, :D_C])
    ig = jax.nn.sigmoid(g[:, D_C:])
    half_log_a = (0.5 * RG_C) * r * _log_sigmoid(lam)
    a = jnp.exp(2.0 * half_log_a)
    z = -jnp.tanh(half_log_a)
    root = jnp.where(z > 0.0, z * lax.rsqrt(z), 0.0)
    b = (1.0 + a) * root * ig * xc
    return a, b


def _compose_step(a, b, shift):
    b = a * pltpu.roll(b, shift, axis=0) + b
    a = a * pltpu.roll(a, shift, axis=0)
    return a, b


def _scan_groups(a, b, h0_rows):
    row = lax.broadcasted_iota(jnp.int32, a.shape, 0) % SUBLANES
    for s in (1, 2, 4):
        keep = row >= s
        a_sh = jnp.where(keep, pltpu.roll(a, s, axis=0), 1.0)
        b_sh = jnp.where(keep, pltpu.roll(b, s, axis=0), 0.0)
        b = a * b_sh + b
        a = a * a_sh
    return a * h0_rows + b


def _scan_tile(a, b, h_last):
    n, c = a.shape
    a = jnp.concatenate([jnp.ones((SUBLANES, c), F32), a], axis=0)
    b = jnp.concatenate([jnp.zeros((SUBLANES, c), F32), b], axis=0)
    for s in (1, 2, 4):
        a, b = _compose_step(a, b, s)
    h = jnp.broadcast_to(h_last, (SUBLANES, c))
    out = []
    for j in range(1, n // SUBLANES + 1):
        rows = slice(j * SUBLANES, (j + 1) * SUBLANES)
        h = a[rows] * h + b[rows]
        out.append(h)
    return jnp.concatenate(out, axis=0)


def _prompt_proj_kernel(x_ref, g1_ref, win_ref, cr_ref, sr_ref, crs_ref, srs_ref, cb_ref, sb_ref,
                        wpool_ref, pscale_ref, convw_ref, convb_ref, wg_ref, bg_ref, lam_ref,
                        q_ref, k_ref, v_ref, obc_ref, pst_ref, cst_ref, hst_ref,
                        ubuf, xbuf, hcar):
    i = pl.program_id(0)
    tm = x_ref.shape[0]

    @pl.when(i == 0)
    def _():
        ubuf[0:POOL_MAX, :] = jnp.zeros((POOL_MAX, D_B), F32)
        xbuf[0:SUBLANES, :] = jnp.zeros((SUBLANES, D_C), F32)
        hcar[...] = jnp.zeros((SUBLANES, D_C), F32)

    def project(r0):
        rows = slice(r0, r0 + PROJ_SUB)
        proj = _project(x_ref[rows, :], g1_ref[...], win_ref[...])
        sub = i * (tm // PROJ_SUB) + r0 // PROJ_SUB
        cb = cb_ref[pl.ds(sub, 1), :]
        sb = sb_ref[pl.ds(sub, 1), :]
        cos = cb * cr_ref[...] - sb * sr_ref[...]
        sin = sb * crs_ref[...] + cb * srs_ref[...]
        for hp in range(D_A // LANES):
            sl = slice(hp * LANES, (hp + 1) * LANES)
            q_ref[hp, rows, :] = _rope_tile(proj[:, _Q0:_K0][:, sl], cos, sin)
            k_ref[hp, rows, :] = _rope_tile(proj[:, _K0:_V0][:, sl], cos, sin)
            v_ref[hp, rows, :] = proj[:, _V0:_U0][:, sl]
        u = proj[:, _U0:_X0]
        xr = proj[:, _X0:_G0]
        ubuf[POOL_MAX + r0:POOL_MAX + r0 + PROJ_SUB, :] = u
        xbuf[SUBLANES + r0:SUBLANES + r0 + PROJ_SUB, :] = xr
        return u, xr, proj[:, _G0:]

    def mix(r0, u, gt, h_last):
        rows = slice(r0, r0 + PROJ_SUB)
        pos1 = (i * tm + r0 + 1 + lax.broadcasted_iota(jnp.int32, (PROJ_SUB, LANES), 0)).astype(F32)
        lo = lax.broadcasted_iota(jnp.int32, (PROJ_SUB, LANES), 1) < HEAD_DIM
        cnt0 = jnp.minimum(pos1, jnp.where(lo, 2.0, 4.0))
        cnt1 = jnp.minimum(pos1, jnp.where(lo, 8.0, 16.0))
        ext = slice(r0, r0 + PROJ_SUB + POOL_MAX)
        s2, s4, _, _ = _window_sums_rolled(ubuf[ext, 0:LANES], POOL_MAX)
        _, _, s8, s16 = _window_sums_rolled(ubuf[ext, LANES:D_B], POOL_MAX)
        o_b = _pool_map((s2, s4), (s8, s16), u, cnt0, cnt1, wpool_ref[...], pscale_ref[...])

        a, b = _rglru_terms(lambda s: xbuf[pl.ds(SUBLANES + r0 - s, PROJ_SUB), :], convw_ref[...],
                            convb_ref[...], wg_ref[...], bg_ref[...], lam_ref[...])
        h = _scan_tile(a, b, h_last)
        obc_ref[rows, 0:D_B] = o_b
        obc_ref[rows, D_B:] = h * jax.nn.gelu(gt)
        return h

    h_last = hcar[SUBLANES - 1:SUBLANES, :]
    starts = list(range(0, tm, PROJ_SUB))
    pending = [project(r0) for r0 in starts[:PROJ_AHEAD]]
    for k, r0 in enumerate(starts):
        u, xr, gt = pending.pop(0)
        if k + PROJ_AHEAD < len(starts):
            pending.append(project(starts[k + PROJ_AHEAD]))
        h = mix(r0, u, gt, h_last)
        h_last = h[PROJ_SUB - 1:, :]

    ubuf[0:POOL_MAX, :] = u[PROJ_SUB - POOL_MAX:, :]
    xbuf[0:SUBLANES, :] = xr[PROJ_SUB - SUBLANES:, :]
    hcar[...] = h[PROJ_SUB - SUBLANES:, :]
    pst_ref[...] = u[PROJ_SUB - POOL_MAX:, :]
    cst_ref[...] = xr[PROJ_SUB - SUBLANES:, :]
    hst_ref[...] = h[PROJ_SUB - SUBLANES:, :]


def _const_spec(shape):
    return pl.BlockSpec(shape, lambda *_: (0,) * len(shape))


def _layer_spec(layer, shape):
    return pl.BlockSpec((None,) + tuple(shape), lambda *_: (layer,) + (0,) * len(shape),
                        pipeline_mode=pl.Buffered(1))


def _prompt_proj(x, layer, pw, rope, tm):
    t = x.shape[0]
    n_hp = D_A // LANES
    qkv_shape = jax.ShapeDtypeStruct((n_hp, t, LANES), F32)
    qkv_spec = pl.BlockSpec((n_hp, tm, LANES), lambda i: (0, i, 0))
    return pl.pallas_call(
        _prompt_proj_kernel,
        grid=(t // tm,),
        in_specs=([pl.BlockSpec((tm, D_MODEL), lambda i: (i, 0))]
                  + [_layer_spec(layer, pw[k].shape[1:]) for k in _PROJ_PARAMS]
                  + [_const_spec(r.shape) for r in rope]
                  + [_layer_spec(layer, pw[k].shape[1:]) for k in _MIX_PARAMS]),
        out_specs=[
            qkv_spec, qkv_spec, qkv_spec,
            pl.BlockSpec((tm, D_B + D_C), lambda i: (i, 0)),
            _const_spec((POOL_MAX, D_B)),
            _const_spec((SUBLANES, D_C)),
            _const_spec((SUBLANES, D_C)),
        ],
        out_shape=[
            qkv_shape, qkv_shape, qkv_shape,
            jax.ShapeDtypeStruct((t, D_B + D_C), F32),
            jax.ShapeDtypeStruct((POOL_MAX, D_B), F32),
            jax.ShapeDtypeStruct((SUBLANES, D_C), F32),
            jax.ShapeDtypeStruct((SUBLANES, D_C), F32),
        ],
        scratch_shapes=[
            pltpu.VMEM((POOL_MAX + tm, D_B), F32),
            pltpu.VMEM((SUBLANES + tm, D_C), F32),
            pltpu.VMEM((SUBLANES, D_C), F32),
        ],
        compiler_params=pltpu.CompilerParams(dimension_semantics=("arbitrary",), vmem_limit_bytes=VMEM_LIMIT),
        name="prompt_proj",
    )(x, *[pw[k] for k in _PROJ_PARAMS], *rope, *[pw[k] for k in _MIX_PARAMS])


def _band_bias(dil):
    slabs = MAX_DIL // dil
    rows = BAND // slabs
    row = lax.broadcasted_iota(jnp.int32, (BAND, 2 * BAND), 0)
    col = lax.broadcasted_iota(jnp.int32, (BAND, 2 * BAND), 1)
    qi = slabs * (row % rows) + row // rows
    ki = slabs * (col % (2 * rows)) + col // (2 * rows)
    valid = (ki >= qi) & (ki <= qi + BAND)
    bias = jnp.where(valid, 0.0, NEG)
    bias_first = jnp.where(valid & (ki >= BAND), 0.0, NEG)
    return bias, bias_first


def _prompt_attn_kernel(q_ref, k_ref, v_ref, o_ref, kt_ref, vt_ref, q16, kcat, vcat, acc_s, m_s, l_s):
    i = pl.program_id(1)

    @pl.when(i == pl.num_programs(1) - 1)
    def _():
        kt_ref[...] = k_ref[...].T
        vt_ref[...] = v_ref[...].T

    @pl.when(i == 0)
    def _():
        kcat[0:BAND, :] = jnp.zeros((BAND, MAX_DIL * LANES), F32)
        vcat[0:BAND, :] = jnp.zeros((BAND, MAX_DIL * LANES), F32)

    @pl.when(i > 0)
    def _():
        kcat[0:BAND, :] = kcat[BAND:, :]
        vcat[0:BAND, :] = vcat[BAND:, :]

    for r in range(MAX_DIL):
        ls = slice(r * LANES, (r + 1) * LANES)
        q16[:, ls] = q_ref[pl.ds(r, BAND, stride=MAX_DIL), :]
        kcat[BAND:, ls] = k_ref[pl.ds(r, BAND, stride=MAX_DIL), :]
        vcat[BAND:, ls] = v_ref[pl.ds(r, BAND, stride=MAX_DIL), :]

    head0 = lax.broadcasted_iota(jnp.int32, (BAND, LANES), 1) < HEAD_DIM
    ones_cols = jnp.ones((2 * BAND, LANES), BF16)

    units = []
    for stage, (_, dil) in enumerate(DILATED_PATTERNS):
        slabs = MAX_DIL // dil
        rows = BAND // slabs
        bias, bias_first = _band_bias(dil)
        bias_blk0 = jnp.where(i == 0, bias_first, bias)
        for blk in range(slabs):
            for res in range(dil):
                units.append((stage, slabs, rows, blk, [res + dil * c for c in range(slabs)],
                              bias_blk0 if blk == 0 else bias))

    def scores(unit):
        _, _, rows, blk, groups, _ = unit
        q_rows = slice(rows * blk, rows * (blk + 1))
        k_rows = slice(BAND + rows * (blk - 1), BAND + rows * (blk + 1))
        lanes = [slice(g * LANES, (g + 1) * LANES) for g in groups]
        qb = jnp.concatenate([q16[q_rows, ls] for ls in lanes], axis=0)
        kk = jnp.concatenate([kcat[k_rows, ls] for ls in lanes], axis=0).astype(BF16)
        vv = jnp.concatenate([vcat[k_rows, ls] for ls in lanes], axis=0).astype(BF16)
        q2 = jnp.concatenate([jnp.where(head0, qb, 0.0), jnp.where(head0, 0.0, qb)], axis=0).astype(BF16)
        s2 = lax.dot_general(q2, kk, (((1,), (1,)), ((), ())), preferred_element_type=F32)
        return s2, jnp.concatenate([vv, ones_cols], axis=1)

    def finish(unit, s2, vv1):
        stage, _, rows, blk, groups, b_u = unit
        q_rows = slice(rows * blk, rows * (blk + 1))
        lanes = [slice(g * LANES, (g + 1) * LANES) for g in groups]
        ms, ps = [], []
        for hh in range(2):
            s = s2[hh * BAND:(hh + 1) * BAND] + b_u
            m = jnp.max(s, axis=1, keepdims=True)
            ms.append(m)
            ps.append(jnp.exp((s - m).astype(BF16)))
        pv2 = jnp.dot(jnp.concatenate(ps, axis=0), vv1, preferred_element_type=F32)
        parts = [(ms[hh], pv2[hh * BAND:(hh + 1) * BAND, LANES:], pv2[hh * BAND:(hh + 1) * BAND, :LANES])
                 for hh in range(2)]
        m_b = jnp.where(head0, parts[0][0], parts[1][0])
        l_b = jnp.where(head0, parts[0][1], parts[1][1])
        pv_b = jnp.where(head0, parts[0][2], parts[1][2])
        if stage > 0:
            m_o = jnp.concatenate([m_s[q_rows, ls] for ls in lanes], axis=0)
            l_o = jnp.concatenate([l_s[q_rows, ls] for ls in lanes], axis=0)
            a_o = jnp.concatenate([acc_s[q_rows, ls] for ls in lanes], axis=0)
            m_n = jnp.maximum(m_o, m_b)
            w_o = jnp.exp(m_o - m_n)
            w_b = jnp.exp(m_b - m_n)
            l_b = l_o * w_o + l_b * w_b
            pv_b = a_o * w_o + pv_b * w_b
            m_b = m_n
        last = stage == len(DILATED_PATTERNS) - 1
        if last:
            out = pv_b / l_b
        for c, ls in enumerate(lanes):
            sub = slice(c * rows, (c + 1) * rows)
            if last:
                o_ref[pl.ds(MAX_DIL * rows * blk + groups[c], rows, stride=MAX_DIL), :] = out[sub]
            else:
                m_s[q_rows, ls] = m_b[sub]
                l_s[q_rows, ls] = l_b[sub]
                acc_s[q_rows, ls] = pv_b[sub]

    for unit in units:
        finish(unit, *scores(unit))


def _prompt_attn(q, k, v):
    n_hp, t, _ = q.shape
    n_super = t // SUPER
    width = MAX_DIL * LANES
    cur = pl.BlockSpec((None, SUPER, LANES), lambda h, i: (h, i, 0))
    win = pl.BlockSpec((None, LANES, SUPER), lambda h, i: (h, 0, 0))
    win_shape = jax.ShapeDtypeStruct((n_hp, LANES, SUPER), F32)
    return pl.pallas_call(
        _prompt_attn_kernel,
        grid=(n_hp, n_super),
        in_specs=[cur, cur, cur],
        out_specs=[cur, win, win],
        out_shape=[jax.ShapeDtypeStruct(q.shape, F32), win_shape, win_shape],
        scratch_shapes=[
            pltpu.VMEM((BAND, width), F32),
            pltpu.VMEM((2 * BAND, width), F32),
            pltpu.VMEM((2 * BAND, width), F32),
            pltpu.VMEM((BAND, width), F32),
            pltpu.VMEM((BAND, width), F32),
            pltpu.VMEM((BAND, width), F32),
        ],
        compiler_params=pltpu.CompilerParams(dimension_semantics=("arbitrary", "arbitrary"),
                                             vmem_limit_bytes=VMEM_LIMIT),
        name="prompt_attn",
    )(q, k, v)


FF_CHUNK = 256


def _outffn_begin(x_ref, oa_ref, obc_ref, wout_ref, g2_ref):
    mix = jnp.concatenate([oa_ref[c] for c in range(oa_ref.shape[0])] + [obc_ref[...]], axis=1).astype(BF16)
    x1 = x_ref[...] + jnp.dot(mix, wout_ref[...], preferred_element_type=F32)
    return x1, _rms(x1, g2_ref[...]).astype(BF16)


def _ffn_chunks(h2, acc, wgu_ref, wdown_ref, chunks):
    for c in chunks:
        g = jnp.dot(h2, wgu_ref[:, c * FF_CHUNK:(c + 1) * FF_CHUNK], preferred_element_type=F32)
        up = jnp.dot(h2, wgu_ref[:, D_FF + c * FF_CHUNK:D_FF + (c + 1) * FF_CHUNK], preferred_element_type=F32)
        act = (jax.nn.silu(g) * up).astype(BF16)
        acc = acc + jnp.dot(act, wdown_ref[c * FF_CHUNK:(c + 1) * FF_CHUNK, :], preferred_element_type=F32)
    return acc


def _outffn_rows(x_ref, oa_ref, obc_ref, wout_ref, g2_ref, wgu_ref, wdown_ref, fg_ref, final):
    x1, h2 = _outffn_begin(x_ref, oa_ref, obc_ref, wout_ref, g2_ref)
    acc = _ffn_chunks(h2, x1, wgu_ref, wdown_ref, range(D_FF // FF_CHUNK))
    return _rms(acc, fg_ref[...]) if final else acc


def _outffn_kernel(x_ref, oa_ref, obc_ref, wout_ref, g2_ref, wgu_ref, wdown_ref, fg_ref, o_ref, *, final):
    o_ref[...] = _outffn_rows(x_ref, oa_ref, obc_ref, wout_ref, g2_ref, wgu_ref, wdown_ref, fg_ref, final)


def _outffn_specs(oa, layer, pw, tm, index):
    return [
        pl.BlockSpec((tm, D_MODEL), lambda *g: (index(*g), 0)),
        pl.BlockSpec((oa.shape[0], tm, oa.shape[2]), lambda *g: (0, index(*g), 0)),
        pl.BlockSpec((tm, D_B + D_C), lambda *g: (index(*g), 0)),
    ] + [_layer_spec(layer, pw[k].shape[1:]) for k in _FFN_PARAMS] + [_const_spec((1, D_MODEL))]


def _outffn(x, oa, obc, layer, pw, final_g, final, tm):
    t = x.shape[0]
    return pl.pallas_call(
        functools.partial(_outffn_kernel, final=final),
        grid=(t // tm,),
        in_specs=_outffn_specs(oa, layer, pw, tm, lambda i: i),
        out_specs=pl.BlockSpec((tm, D_MODEL), lambda i: (i, 0)),
        out_shape=jax.ShapeDtypeStruct((t, D_MODEL), F32),
        compiler_params=pltpu.CompilerParams(dimension_semantics=("parallel",), vmem_limit_bytes=VMEM_LIMIT),
        name="outffn",
    )(x, oa, obc, *[pw[k] for k in _FFN_PARAMS], final_g)


def _sample_proj_kernel(x_ref, g1_ref, win_ref, cos_ref, sin_ref, wpool_ref, pscale_ref, convw_ref, convb_ref,
                        wg_ref, bg_ref, lam_ref, spool_ref, sconv_ref, h0_ref,
                        q_ref, k_ref, v_ref, obc_ref, npool_ref, nconv_ref, nh_ref,
                        ubuf, xbuf):
    n = x_ref.shape[0]
    nb = n // SUBLANES
    proj = _project(x_ref[...], g1_ref[...], win_ref[...])
    cos = cos_ref[...]
    sin = sin_ref[...]
    for c in range(HEAD_SPLIT):
        sl = slice(c * D_A_BLK, (c + 1) * D_A_BLK)
        q_ref[c] = _rope_tile(proj[:, _Q0:_K0][:, sl], cos, sin)
        k_ref[c] = _rope_tile(proj[:, _K0:_V0][:, sl], cos, sin)
        v_ref[c] = proj[:, _V0:_U0][:, sl]

    u = proj[:, _U0:_X0]
    xr = proj[:, _X0:_G0]
    gt = proj[:, _G0:]
    ubuf[:, 0:POOL_MAX, :] = spool_ref[...]
    ubuf[:, POOL_MAX:, :] = u.reshape(nb, SUBLANES, D_B)
    xbuf[:, 0:SUBLANES, :] = sconv_ref[...]
    xbuf[:, SUBLANES:, :] = xr.reshape(nb, SUBLANES, D_C)

    lo = lax.broadcasted_iota(jnp.int32, (n, LANES), 1) < HEAD_DIM
    cnt0 = jnp.where(lo, 2.0, 4.0)
    cnt1 = jnp.where(lo, 8.0, 16.0)
    sums0 = _window_sums_shifted(lambda s: ubuf[:, pl.ds(POOL_MAX - s, SUBLANES), 0:LANES].reshape(n, LANES), 2, 4)
    sums1 = _window_sums_shifted(lambda s: ubuf[:, pl.ds(POOL_MAX - s, SUBLANES), LANES:D_B].reshape(n, LANES), 8, 16)
    o_b = _pool_map(sums0, sums1, u, cnt0, cnt1, wpool_ref[...], pscale_ref[...])

    a, b = _rglru_terms(lambda s: xbuf[:, pl.ds(SUBLANES - s, SUBLANES), :].reshape(n, D_C), convw_ref[...],
                        convb_ref[...], wg_ref[...], bg_ref[...], lam_ref[...])
    h = _scan_groups(a, b, h0_ref[...])
    obc_ref[:, 0:D_B] = o_b
    obc_ref[:, D_B:] = h * jax.nn.gelu(gt)
    npool_ref[...] = ubuf[:, SUBLANES:, :]
    nconv_ref[...] = xr.reshape(nb, SUBLANES, D_C)
    nh_ref[...] = h


def _sample_proj(x, layer, pw, cos, sin, spool, sconv, h0rows):
    n = x.shape[0]
    nb = n // SUBLANES
    shapes = [
        jax.ShapeDtypeStruct((HEAD_SPLIT, n, D_A_BLK), F32), jax.ShapeDtypeStruct((HEAD_SPLIT, n, D_A_BLK), F32),
        jax.ShapeDtypeStruct((HEAD_SPLIT, n, D_A_BLK), F32),
        jax.ShapeDtypeStruct((n, D_B + D_C), F32),
        jax.ShapeDtypeStruct((nb, POOL_MAX, D_B), F32),
        jax.ShapeDtypeStruct((nb, SUBLANES, D_C), F32),
        jax.ShapeDtypeStruct((n, D_C), F32),
    ]
    data = (spool, sconv, h0rows)
    in_specs = ([_const_spec(x.shape)] + [_layer_spec(layer, pw[k].shape[1:]) for k in _PROJ_PARAMS]
                + [_const_spec(cos.shape), _const_spec(sin.shape)]
                + [_layer_spec(layer, pw[k].shape[1:]) for k in _MIX_PARAMS] + [_const_spec(a.shape) for a in data])
    args = (x, *[pw[k] for k in _PROJ_PARAMS], cos, sin, *[pw[k] for k in _MIX_PARAMS], *data)
    return pl.pallas_call(
        _sample_proj_kernel,
        grid=(1,),
        in_specs=in_specs,
        out_specs=[_const_spec(s.shape) for s in shapes],
        out_shape=shapes,
        scratch_shapes=[
            pltpu.VMEM((nb, POOL_MAX + SUBLANES, D_B), F32),
            pltpu.VMEM((nb, 2 * SUBLANES, D_C), F32),
        ],
        compiler_params=pltpu.CompilerParams(dimension_semantics=("arbitrary",), vmem_limit_bytes=VMEM_LIMIT),
        name="sample_proj",
    )(*args)


def _sample_key_counts(w_buf, t_new, n_heads):
    k = np.arange(w_buf + LANES)[None, :]
    dist = w_buf + np.arange(t_new)[:, None] - k
    cnt = np.zeros(dist.shape, np.float32)
    for window, dil in DILATED_PATTERNS:
        cnt += ((dist >= 0) & (dist <= window) & (dist % dil == 0)).astype(np.float32)
    return np.tile(cnt, (n_heads, 1))


def _shift_window(c_ref, new_ref, s_ref):
    da, w_buf = c_ref.shape
    t_new = new_ref.shape[0]
    tail = lax.broadcasted_iota(jnp.int32, (da, LANES), 1) >= LANES - t_new
    rolled = pltpu.roll(c_ref[...], w_buf - t_new, axis=1)
    new_t = jnp.concatenate([jnp.zeros((LANES - t_new, da), F32), new_ref[...]], axis=0).T
    s_ref[:, 0:w_buf - LANES] = rolled[:, 0:w_buf - LANES]
    s_ref[:, w_buf - LANES:] = jnp.where(tail, new_t, rolled[:, w_buf - LANES:])


def _sample_attn_scores(ck_ref, q_ref, kn_ref):
    da = ck_ref.shape[0]
    t_new = q_ref.shape[0]
    qt = jnp.concatenate([q_ref[...]] * (da // HEAD_DIM), axis=0)
    row = lax.broadcasted_iota(jnp.int32, qt.shape, 0)
    lane = lax.broadcasted_iota(jnp.int32, qt.shape, 1)
    own = (row // t_new) == (lane // HEAD_DIM)
    qbd = jnp.where(own, qt, 0.0).astype(BF16)
    kn_pad = jnp.concatenate([kn_ref[...], jnp.zeros((LANES - t_new, da), F32)], axis=0).astype(BF16)
    return jnp.concatenate([
        jnp.dot(qbd, ck_ref[...].astype(BF16), preferred_element_type=F32),
        lax.dot_general(qbd, kn_pad, (((1,), (1,)), ((), ())), preferred_element_type=F32)], axis=1)


def _sample_attn_finish(s, cnt_ref, cv_ref, vn_ref, oa_ref):
    da, w_buf = cv_ref.shape
    t_new = vn_ref.shape[0]
    cv = cv_ref[...]
    vn = vn_ref[...]
    zpad = jnp.zeros((LANES - t_new, da), F32)

    cnt = cnt_ref[...]
    s = jnp.where(cnt > 0.0, s, NEG)
    m = jnp.max(s, axis=1, keepdims=True)
    p = jnp.exp(s - m) * cnt
    den = jnp.sum(p, axis=1, keepdims=True)
    pb = p.astype(BF16)
    vn_pad = jnp.concatenate([vn, zpad], axis=0).astype(BF16)
    o_n = (lax.dot_general(pb[:, :w_buf], cv.astype(BF16), (((1,), (1,)), ((), ())), preferred_element_type=F32)
           + jnp.dot(pb[:, w_buf:], vn_pad, preferred_element_type=F32)) / den
    out = jnp.zeros((t_new, da), F32)
    lane_o = lax.broadcasted_iota(jnp.int32, (t_new, da), 1) // HEAD_DIM
    for h in range(da // HEAD_DIM):
        out = jnp.where(lane_o == h, o_n[h * t_new:(h + 1) * t_new, :], out)
    oa_ref[...] = out


def _outffn_sample_attn_kernel(x_ref, oa_ref, obc_ref, wout_ref, g2_ref, wgu_ref, wdown_ref, fg_ref,
                               cnt_ref, ck_ref, cv_ref, q_ref, kn_ref, vn_ref, *rest, final, parts):
    o_ref, sk_ref, sv_ref, oas_ref, acc_scr, h2_scr = rest[-6:]

    for part, chunks in enumerate(parts):
        @pl.when(pl.program_id(1) == part)
        def _(part=part, chunks=chunks):
            scores = _sample_attn_scores(ck_ref, q_ref, kn_ref)
            _shift_window(ck_ref, kn_ref, sk_ref)
            _shift_window(cv_ref, vn_ref, sv_ref)
            if part == 0:
                acc, h2 = _outffn_begin(x_ref, oa_ref, obc_ref, wout_ref, g2_ref)
                h2_scr[...] = h2
            else:
                acc, h2 = acc_scr[...], h2_scr[...]
            acc = _ffn_chunks(h2, acc, wgu_ref, wdown_ref, chunks)
            if part == len(parts) - 1:
                o_ref[...] = _rms(acc, fg_ref[...]) if final else acc
            else:
                acc_scr[...] = acc
            _sample_attn_finish(scores, cnt_ref, cv_ref, vn_ref, oas_ref)


def _outffn_sample_attn(x, oa, obc, layer, pw, final_g, final, tm, cache_k, cache_v, q, kn, vn, win_bufs, counts):
    t = x.shape[0]
    n_tiles = t // tm
    depth, nb, split, da, w_buf = cache_k.shape
    t_new = q.shape[1] // nb
    per_tile = nb * split // n_tiles
    assert per_tile * n_tiles == nb * split
    parts = [tuple(int(c) for c in p) for p in np.array_split(np.arange(D_FF // FF_CHUNK), per_tile)[::-1]]

    def group(i, j):
        g = i * per_tile + j
        return g // split, g % split

    def cache_index(i, j):
        b, c = group(i, j)
        return (layer, b, c, 0, 0)

    def tok_index(i, j):
        b, c = group(i, j)
        return (c, b, 0)

    cache_spec = pl.BlockSpec((None, None, None, da, w_buf), cache_index)
    tok_spec = pl.BlockSpec((None, t_new, da), tok_index)
    in_specs = _outffn_specs(oa, layer, pw, tm, lambda i, j: i) + [
        _const_spec(counts.shape), cache_spec, cache_spec, tok_spec, tok_spec, tok_spec]
    args = [x, oa, obc, *[pw[k] for k in _FFN_PARAMS], final_g, counts, cache_k, cache_v, q, kn, vn]
    aliases = {}
    if win_bufs is not None:
        in_specs += [pl.BlockSpec(memory_space=pl.ANY)] * 2
        aliases = {len(args): 1, len(args) + 1: 2}
        args += list(win_bufs)
    win_shape = jax.ShapeDtypeStruct(cache_k.shape, F32)
    y, sk, sv, oa_s = pl.pallas_call(
        functools.partial(_outffn_sample_attn_kernel, final=final, parts=parts),
        grid=(n_tiles, per_tile),
        in_specs=in_specs,
        out_specs=[pl.BlockSpec((tm, D_MODEL), lambda i, j: (i, 0)), cache_spec, cache_spec, tok_spec],
        out_shape=[jax.ShapeDtypeStruct((t, D_MODEL), F32), win_shape, win_shape, jax.ShapeDtypeStruct(q.shape, F32)],
        input_output_aliases=aliases,
        scratch_shapes=[pltpu.VMEM((tm, D_MODEL), F32), pltpu.VMEM((tm, D_MODEL), BF16)],
        compiler_params=pltpu.CompilerParams(dimension_semantics=("arbitrary", "arbitrary"),
                                             vmem_limit_bytes=VMEM_LIMIT),
        name="outffn_sample_attn",
    )(*args)
    return y, (sk, sv), oa_s


def _rope_tables(pos):
    half = HEAD_DIM // 2
    inv = jnp.power(ROPE_THETA, -2.0 * jnp.arange(half, dtype=F32) / HEAD_DIM)
    ang = pos.astype(F32)[:, None] * inv[None, :]
    cos = jnp.tile(jnp.cos(ang), (1, LANES // half))
    sin = jnp.sin(ang)
    sin_signed = jnp.tile(jnp.concatenate([-sin, sin], axis=1), (1, LANES // HEAD_DIM))
    return cos, sin_signed


def _rope_tile_tables(seq):
    half = HEAD_DIM // 2
    inv = jnp.power(ROPE_THETA, -2.0 * jnp.arange(half, dtype=F32) / HEAD_DIM)
    inv = jnp.tile(inv, LANES // half)[None, :]
    sign = jnp.tile(jnp.concatenate([-jnp.ones(half, F32), jnp.ones(half, F32)]), LANES // HEAD_DIM)[None, :]
    ang_r = jnp.arange(PROJ_SUB, dtype=F32)[:, None] * inv
    ang_b = (jnp.arange(seq // PROJ_SUB, dtype=F32) * PROJ_SUB)[:, None] * inv
    cr, sr = jnp.cos(ang_r), jnp.sin(ang_r)
    return cr, sr, sign * cr, sign * sr, jnp.cos(ang_b), jnp.sin(ang_b)


def _block_diag(w):
    depth, n, c, d = w.shape
    eye = jnp.eye(n, dtype=w.dtype)
    return (eye[None, :, None, :, None] * w[:, :, :, None, :]).reshape(depth, n * c, n * d)


_PROJ_PARAMS = ("g1", "w_in")
_MIX_PARAMS = ("wpool", "pscale", "convw", "convb", "wg", "bg", "lam")
_FFN_PARAMS = ("w_out", "g2", "w_gu", "w_down")


def _stack_params(norm1_g, w_in, pool_w, pool_scale, conv_w, conv_b, gate_a_w, gate_a_b, gate_x_w, gate_x_b, lru_lambda,
                  w_out, norm2_g, w_gu, w_down):
    q_scale = jnp.where(jnp.arange(D_IN) < D_A, HEAD_DIM ** -0.5, 1.0).astype(F32)
    return dict(
        g1=norm1_g[:, None, :],
        w_in=(w_in * q_scale).astype(BF16),
        wpool=_block_diag(pool_w).astype(BF16),
        pscale=pool_scale[:, None, :],
        convw=conv_w,
        convb=conv_b[:, None, :],
        wg=jnp.concatenate([_block_diag(gate_a_w), _block_diag(gate_x_w)], axis=2).astype(BF16),
        bg=jnp.concatenate([gate_a_b, gate_x_b], axis=1)[:, None, :],
        lam=lru_lambda[:, None, :],
        w_out=w_out.astype(BF16),
        g2=norm2_g[:, None, :],
        w_gu=w_gu.astype(BF16),
        w_down=w_down.astype(BF16),
    )


def kernel(x_prompt, x_sample, cache_win_k, cache_win_v, state_pool, state_conv, state_rglru, norm1_g, w_in, pool_w,
           pool_scale, conv_w, conv_b, gate_a_w, gate_a_b, gate_x_w, gate_x_b, lru_lambda, w_out, norm2_g, w_gu,
           w_down, final_g):
    batch, seq, _ = x_prompt.shape
    nb, t_new, _ = x_sample.shape
    depth, _, w_buf = cache_win_k.shape[:3]
    assert batch == 1 and seq % SUPER == 0 and seq % PROJ_TILE == 0 and seq % FFN_TILE == 0
    assert t_new == SUBLANES and w_buf == W_MAX and SUPER == W_MAX

    rope_p = _rope_tile_tables(seq)
    cos_s, sin_s = _rope_tables(PAST_LEN + jnp.arange(t_new, dtype=jnp.int32))
    cos_s = jnp.tile(cos_s, (nb, 1))
    sin_s = jnp.tile(sin_s, (nb, 1))
    counts = jnp.asarray(_sample_key_counts(w_buf, t_new, D_A_BLK // HEAD_DIM))
    fg = final_g[None, :]

    ck = jnp.transpose(cache_win_k, (0, 1, 3, 4, 2)).reshape(depth, nb, HEAD_SPLIT, D_A_BLK, w_buf)
    cv = jnp.transpose(cache_win_v, (0, 1, 3, 4, 2)).reshape(depth, nb, HEAD_SPLIT, D_A_BLK, w_buf)
    spool = jnp.pad(state_pool, ((0, 0), (0, 0), (1, 0), (0, 0)))
    sconv = jnp.pad(state_conv, ((0, 0), (0, 0), (SUBLANES - (CONV_W - 1), 0), (0, 0)))
    h0rows = jnp.repeat(state_rglru, t_new, axis=1)

    hp = x_prompt.reshape(seq, D_MODEL)
    hs = x_sample.reshape(nb * t_new, D_MODEL)
    pw = _stack_params(norm1_g, w_in, pool_w, pool_scale, conv_w, conv_b, gate_a_w, gate_a_b, gate_x_w, gate_x_b,
                       lru_lambda, w_out, norm2_g, w_gu, w_down)
    win_bufs = None
    p_k, p_v, p_pool, p_conv, p_h, s_pool, s_conv, s_h = [], [], [], [], [], [], [], []
    for l in range(depth):
        last = l == depth - 1
        qs, kn, vn, obc_s, npool, nconv, nh = _sample_proj(hs, l, pw, cos_s, sin_s, spool[l], sconv[l], h0rows[l])
        q, k, v, obc, pst, cst, hst = _prompt_proj(hp, l, pw, rope_p, PROJ_TILE)
        oa, kt, vt = _prompt_attn(q, k, v)
        hp, win_bufs, oa_s = _outffn_sample_attn(hp, oa, obc, l, pw, fg, last, FFN_TILE, ck, cv, qs, kn, vn,
                                                 win_bufs, counts)
        p_k.append(kt)
        p_v.append(vt)
        p_pool.append(pst[None, 1:, :])
        p_conv.append(cst[None, SUBLANES - (CONV_W - 1):, :])
        p_h.append(hst[SUBLANES - 1:, :])
        hs = _outffn(hs, oa_s, obc_s, l, pw, fg, last, nb * t_new)
        s_pool.append(npool[:, 1:, :])
        s_conv.append(nconv[:, SUBLANES - (CONV_W - 1):, :])
        s_h.append(nh.reshape(nb, t_new, D_C)[:, t_new - 1, :])

    y_prompt = hp.reshape(batch, seq, D_MODEL)
    y_sample = hs.reshape(nb, t_new, D_MODEL)
    s_win_k = jnp.transpose(win_bufs[0].reshape(depth, nb, N_HEADS_A, HEAD_DIM, w_buf), (0, 1, 4, 2, 3))
    s_win_v = jnp.transpose(win_bufs[1].reshape(depth, nb, N_HEADS_A, HEAD_DIM, w_buf), (0, 1, 4, 2, 3))
    p_win_k = jnp.transpose(jnp.stack(p_k).reshape(depth, 1, N_HEADS_A, HEAD_DIM, W_MAX), (0, 1, 4, 2, 3))
    p_win_v = jnp.transpose(jnp.stack(p_v).reshape(depth, 1, N_HEADS_A, HEAD_DIM, W_MAX), (0, 1, 4, 2, 3))
    return (y_prompt, y_sample, p_win_k, p_win_v, jnp.stack(p_pool), jnp.stack(p_conv),
            jnp.stack(p_h), s_win_k, s_win_v, jnp.stack(s_pool), jnp.stack(s_conv), jnp.stack(s_h))
```

```python
import functools

import numpy as np
import jax
import jax.numpy as jnp
from jax import lax
from jax.experimental import pallas as pl
from jax.experimental.pallas import tpu as pltpu

F32 = jnp.float32
BF16 = jnp.bfloat16

D_MODEL = 1024
HEAD_DIM = 64
D_A = 384
N_HEADS_A = D_A // HEAD_DIM
D_B = 256
D_C = 384
D_IN = 3 * D_A + D_B + 2 * D_C
D_FF = 2816
POOL_MAX = 16
CONV_W = 4
RG_C = 8.0
EPS = 1e-6
ROPE_THETA = 10000.0
W_MAX = 2048
PAST_LEN = 16384
DILATED_PATTERNS = ((128, 1), (512, 4), (2048, 16))
BAND = 128
MAX_DIL = 16
SUPER = BAND * MAX_DIL

LANES = 128
SUBLANES = 8
NEG = -1e30
VMEM_LIMIT = 56 * 1024 * 1024

PROJ_TILE = 1024
PROJ_SUB = 256
PROJ_AHEAD = 2
FFN_TILE = 512
HEAD_SPLIT = 3
D_A_BLK = D_A // HEAD_SPLIT

_Q0, _K0, _V0, _U0, _X0, _G0 = 0, D_A, 2 * D_A, 3 * D_A, 3 * D_A + D_B, 3 * D_A + D_B + D_C


def _rms(x, g):
    return x * lax.rsqrt(jnp.mean(x * x, axis=-1, keepdims=True) + EPS) * g


def _rope_tile(t, cos, sin_signed):
    lane = lax.broadcasted_iota(jnp.int32, t.shape, 1)
    first = (lane % HEAD_DIM) < (HEAD_DIM // 2)
    swapped = jnp.where(first, pltpu.roll(t, LANES - HEAD_DIM // 2, axis=1), pltpu.roll(t, HEAD_DIM // 2, axis=1))
    return t * cos + swapped * sin_signed


def _log_sigmoid(x):
    return jnp.minimum(x, 0.0) - jnp.log1p(jnp.exp(-jnp.abs(x)))


def _project(x, g1, w_in):
    hn = _rms(x, g1)
    return jnp.dot(hn.astype(BF16), w_in, preferred_element_type=F32)


def _pool_map(sums0, sums1, u, cnt0, cnt1, wpool, pscale):
    lo = lax.broadcasted_iota(jnp.int32, cnt0.shape, 1) < HEAD_DIM
    mean0 = jnp.where(lo, sums0[0], sums0[1]) / cnt0
    mean1 = jnp.where(lo, sums1[0], sums1[1]) / cnt1
    pooled = jnp.concatenate([mean0, mean1], axis=1) - u
    return jnp.dot(pooled.astype(BF16), wpool, preferred_element_type=F32) * pscale


def _window_sums_shifted(shift, narrow, wide):
    s = shift(0)
    for i in range(1, narrow):
        s = s + shift(i)
    s_narrow = s
    for i in range(narrow, wide):
        s = s + shift(i)
    return s_narrow, s


def _window_sums_rolled(xe, halo):
    s2 = xe + pltpu.roll(xe, 1, axis=0)
    s4 = s2 + pltpu.roll(s2, 2, axis=0)
    s8 = s4 + pltpu.roll(s4, 4, axis=0)
    s16 = s8 + pltpu.roll(s8, 8, axis=0)
    return s2[halo:], s4[halo:], s8[halo:], s16[halo:]


def _rglru_terms(xshift, convw, convb, wg, bg, lam):
    xc = convb + (xshift(3) * convw[0:1] + xshift(2) * convw[1:2] + xshift(1) * convw[2:3] + xshift(0) * convw[3:4])
    g = jnp.dot(xc.astype(BF16), wg, preferred_element_type=F32) + bg
    r = jax.nn.sigmoid(g[:, :D_C])
    ig = jax.nn.sigmoid(g[:, D_C:])
    half_log_a = (0.5 * RG_C) * r * _log_sigmoid(lam)
    a = jnp.exp(2.0 * half_log_a)
    z = -jnp.tanh(half_log_a)
    root = jnp.where(z > 0.0, z * lax.rsqrt(z), 0.0)
    b = (1.0 + a) * root * ig * xc
    return a, b


def _compose_step(a, b, shift):
    b = a * pltpu.roll(b, shift, axis=0) + b
    a = a * pltpu.roll(a, shift, axis=0)
    return a, b


def _scan_groups(a, b, h0_rows):
    row = lax.broadcasted_iota(jnp.int32, a.shape, 0) % SUBLANES
    for s in (1, 2, 4):
        keep = row >= s
        a_sh = jnp.where(keep, pltpu.roll(a, s, axis=0), 1.0)
        b_sh = jnp.where(keep, pltpu.roll(b, s, axis=0), 0.0)
        b = a * b_sh + b
        a = a * a_sh
    return a * h0_rows + b


def _scan_tile(a, b, h_last):
    n, c = a.shape
    a = jnp.concatenate([jnp.ones((SUBLANES, c), F32), a], axis=0)
    b = jnp.concatenate([jnp.zeros((SUBLANES, c), F32), b], axis=0)
    for s in (1, 2, 4):
        a, b = _compose_step(a, b, s)
    h = jnp.broadcast_to(h_last, (SUBLANES, c))
    out = []
    for j in range(1, n // SUBLANES + 1):
        rows = slice(j * SUBLANES, (j + 1) * SUBLANES)
        h = a[rows] * h + b[rows]
        out.append(h)
    return jnp.concatenate(out, axis=0)


def _prompt_proj_kernel(x_ref, g1_ref, win_ref, cr_ref, sr_ref, crs_ref, srs_ref, cb_ref, sb_ref,
                        wpool_ref, pscale_ref, convw_ref, convb_ref, wg_ref, bg_ref, lam_ref,
                        q_ref, k_ref, v_ref, obc_ref, pst_ref, cst_ref, hst_ref,
                        ubuf, xbuf, hcar):
    i = pl.program_id(0)
    tm = x_ref.shape[0]

    @pl.when(i == 0)
    def _():
        ubuf[0:POOL_MAX, :] = jnp.zeros((POOL_MAX, D_B), F32)
        xbuf[0:SUBLANES, :] = jnp.zeros((SUBLANES, D_C), F32)
        hcar[...] = jnp.zeros((SUBLANES, D_C), F32)

    def project(r0):
        rows = slice(r0, r0 + PROJ_SUB)
        proj = _project(x_ref[rows, :], g1_ref[...], win_ref[...])
        sub = i * (tm // PROJ_SUB) + r0 // PROJ_SUB
        cb = cb_ref[pl.ds(sub, 1), :]
        sb = sb_ref[pl.ds(sub, 1), :]
        cos = cb * cr_ref[...] - sb * sr_ref[...]
        sin = sb * crs_ref[...] + cb * srs_ref[...]
        for hp in range(D_A // LANES):
            sl = slice(hp * LANES, (hp + 1) * LANES)
            q_ref[hp, rows, :] = _rope_tile(proj[:, _Q0:_K0][:, sl], cos, sin)
            k_ref[hp, rows, :] = _rope_tile(proj[:, _K0:_V0][:, sl], cos, sin)
            v_ref[hp, rows, :] = proj[:, _V0:_U0][:, sl]
        u = proj[:, _U0:_X0]
        xr = proj[:, _X0:_G0]
        ubuf[POOL_MAX + r0:POOL_MAX + r0 + PROJ_SUB, :] = u
        xbuf[SUBLANES + r0:SUBLANES + r0 + PROJ_SUB, :] = xr
        return u, xr, proj[:, _G0:]

    def mix(r0, u, gt, h_last):
        rows = slice(r0, r0 + PROJ_SUB)
        pos1 = (i * tm + r0 + 1 + lax.broadcasted_iota(jnp.int32, (PROJ_SUB, LANES), 0)).astype(F32)
        lo = lax.broadcasted_iota(jnp.int32, (PROJ_SUB, LANES), 1) < HEAD_DIM
        cnt0 = jnp.minimum(pos1, jnp.where(lo, 2.0, 4.0))
        cnt1 = jnp.minimum(pos1, jnp.where(lo, 8.0, 16.0))
        ext = slice(r0, r0 + PROJ_SUB + POOL_MAX)
        s2, s4, _, _ = _window_sums_rolled(ubuf[ext, 0:LANES], POOL_MAX)
        _, _, s8, s16 = _window_sums_rolled(ubuf[ext, LANES:D_B], POOL_MAX)
        o_b = _pool_map((s2, s4), (s8, s16), u, cnt0, cnt1, wpool_ref[...], pscale_ref[...])

        a, b = _rglru_terms(lambda s: xbuf[pl.ds(SUBLANES + r0 - s, PROJ_SUB), :], convw_ref[...],
                            convb_ref[...], wg_ref[...], bg_ref[...], lam_ref[...])
        h = _scan_tile(a, b, h_last)
        obc_ref[rows, 0:D_B] = o_b
        obc_ref[rows, D_B:] = h * jax.nn.gelu(gt)
        return h

    h_last = hcar[SUBLANES - 1:SUBLANES, :]
    starts = list(range(0, tm, PROJ_SUB))
    pending = [project(r0) for r0 in starts[:PROJ_AHEAD]]
    for k, r0 in enumerate(starts):
        u, xr, gt = pending.pop(0)
        if k + PROJ_AHEAD < len(starts):
            pending.append(project(starts[k + PROJ_AHEAD]))
        h = mix(r0, u, gt, h_last)
        h_last = h[PROJ_SUB - 1:, :]

    ubuf[0:POOL_MAX, :] = u[PROJ_SUB - POOL_MAX:, :]
    xbuf[0:SUBLANES, :] = xr[PROJ_SUB - SUBLANES:, :]
    hcar[...] = h[PROJ_SUB - SUBLANES:, :]
    pst_ref[...] = u[PROJ_SUB - POOL_MAX:, :]
    cst_ref[...] = xr[PROJ_SUB - SUBLANES:, :]
    hst_ref[...] = h[PROJ_SUB - SUBLANES:, :]


def _const_spec(shape):
    return pl.BlockSpec(shape, lambda *_: (0,) * len(shape))


def _layer_spec(layer, shape):
    return pl.BlockSpec((None,) + tuple(shape), lambda *_: (layer,) + (0,) * len(shape),
                        pipeline_mode=pl.Buffered(1))


def _prompt_proj(x, layer, pw, rope, tm):
    t = x.shape[0]
    n_hp = D_A // LANES
    qkv_shape = jax.ShapeDtypeStruct((n_hp, t, LANES), F32)
    qkv_spec = pl.BlockSpec((n_hp, tm, LANES), lambda i: (0, i, 0))
    return pl.pallas_call(
        _prompt_proj_kernel,
        grid=(t // tm,),
        in_specs=([pl.BlockSpec((tm, D_MODEL), lambda i: (i, 0))]
                  + [_layer_spec(layer, pw[k].shape[1:]) for k in _PROJ_PARAMS]
                  + [_const_spec(r.shape) for r in rope]
                  + [_layer_spec(layer, pw[k].shape[1:]) for k in _MIX_PARAMS]),
        out_specs=[
            qkv_spec, qkv_spec, qkv_spec,
            pl.BlockSpec((tm, D_B + D_C), lambda i: (i, 0)),
            _const_spec((POOL_MAX, D_B)),
            _const_spec((SUBLANES, D_C)),
            _const_spec((SUBLANES, D_C)),
        ],
        out_shape=[
            qkv_shape, qkv_shape, qkv_shape,
            jax.ShapeDtypeStruct((t, D_B + D_C), F32),
            jax.ShapeDtypeStruct((POOL_MAX, D_B), F32),
            jax.ShapeDtypeStruct((SUBLANES, D_C), F32),
            jax.ShapeDtypeStruct((SUBLANES, D_C), F32),
        ],
        scratch_shapes=[
            pltpu.VMEM((POOL_MAX + tm, D_B), F32),
            pltpu.VMEM((SUBLANES + tm, D_C), F32),
            pltpu.VMEM((SUBLANES, D_C), F32),
        ],
        compiler_params=pltpu.CompilerParams(dimension_semantics=("arbitrary",), vmem_limit_bytes=VMEM_LIMIT),
        name="prompt_proj",
    )(x, *[pw[k] for k in _PROJ_PARAMS], *rope, *[pw[k] for k in _MIX_PARAMS])


def _band_bias(dil):
    slabs = MAX_DIL // dil
    rows = BAND // slabs
    row = lax.broadcasted_iota(jnp.int32, (BAND, 2 * BAND), 0)
    col = lax.broadcasted_iota(jnp.int32, (BAND, 2 * BAND), 1)
    qi = slabs * (row % rows) + row // rows
    ki = slabs * (col % (2 * rows)) + col // (2 * rows)
    valid = (ki >= qi) & (ki <= qi + BAND)
    bias = jnp.where(valid, 0.0, NEG)
    bias_first = jnp.where(valid & (ki >= BAND), 0.0, NEG)
    return bias, bias_first


def _prompt_attn_kernel(q_ref, k_ref, v_ref, o_ref, kt_ref, vt_ref, q16, kcat, vcat, acc_s, m_s, l_s):
    i = pl.program_id(1)

    @pl.when(i == pl.num_programs(1) - 1)
    def _():
        kt_ref[...] = k_ref[...].T
        vt_ref[...] = v_ref[...].T

    @pl.when(i == 0)
    def _():
        kcat[0:BAND, :] = jnp.zeros((BAND, MAX_DIL * LANES), F32)
        vcat[0:BAND, :] = jnp.zeros((BAND, MAX_DIL * LANES), F32)

    @pl.when(i > 0)
    def _():
        kcat[0:BAND, :] = kcat[BAND:, :]
        vcat[0:BAND, :] = vcat[BAND:, :]

    for r in range(MAX_DIL):
        ls = slice(r * LANES, (r + 1) * LANES)
        q16[:, ls] = q_ref[pl.ds(r, BAND, stride=MAX_DIL), :]
        kcat[BAND:, ls] = k_ref[pl.ds(r, BAND, stride=MAX_DIL), :]
        vcat[BAND:, ls] = v_ref[pl.ds(r, BAND, stride=MAX_DIL), :]

    head0 = lax.broadcasted_iota(jnp.int32, (BAND, LANES), 1) < HEAD_DIM
    ones_cols = jnp.ones((2 * BAND, LANES), BF16)

    units = []
    for stage, (_, dil) in enumerate(DILATED_PATTERNS):
        slabs = MAX_DIL // dil
        rows = BAND // slabs
        bias, bias_first = _band_bias(dil)
        bias_blk0 = jnp.where(i == 0, bias_first, bias)
        for blk in range(slabs):
            for res in range(dil):
                units.append((stage, slabs, rows, blk, [res + dil * c for c in range(slabs)],
                              bias_blk0 if blk == 0 else bias))

    def scores(unit):
        _, _, rows, blk, groups, _ = unit
        q_rows = slice(rows * blk, rows * (blk + 1))
        k_rows = slice(BAND + rows * (blk - 1), BAND + rows * (blk + 1))
        lanes = [slice(g * LANES, (g + 1) * LANES) for g in groups]
        qb = jnp.concatenate([q16[q_rows, ls] for ls in lanes], axis=0)
        kk = jnp.concatenate([kcat[k_rows, ls] for ls in lanes], axis=0).astype(BF16)
        vv = jnp.concatenate([vcat[k_rows, ls] for ls in lanes], axis=0).astype(BF16)
        q2 = jnp.concatenate([jnp.where(head0, qb, 0.0), jnp.where(head0, 0.0, qb)], axis=0).astype(BF16)
        s2 = lax.dot_general(q2, kk, (((1,), (1,)), ((), ())), preferred_element_type=F32)
        return s2, jnp.concatenate([vv, ones_cols], axis=1)

    def finish(unit, s2, vv1):
        stage, _, rows, blk, groups, b_u = unit
        q_rows = slice(rows * blk, rows * (blk + 1))
        lanes = [slice(g * LANES, (g + 1) * LANES) for g in groups]
        ms, ps = [], []
        for hh in range(2):
            s = s2[hh * BAND:(hh + 1) * BAND] + b_u
            m = jnp.max(s, axis=1, keepdims=True)
            ms.append(m)
            ps.append(jnp.exp((s - m).astype(BF16)))
        pv2 = jnp.dot(jnp.concatenate(ps, axis=0), vv1, preferred_element_type=F32)
        parts = [(ms[hh], pv2[hh * BAND:(hh + 1) * BAND, LANES:], pv2[hh * BAND:(hh + 1) * BAND, :LANES])
                 for hh in range(2)]
        m_b = jnp.where(head0, parts[0][0], parts[1][0])
        l_b = jnp.where(head0, parts[0][1], parts[1][1])
        pv_b = jnp.where(head0, parts[0][2], parts[1][2])
        if stage > 0:
            m_o = jnp.concatenate([m_s[q_rows, ls] for ls in lanes], axis=0)
            l_o = jnp.concatenate([l_s[q_rows, ls] for ls in lanes], axis=0)
            a_o = jnp.concatenate([acc_s[q_rows, ls] for ls in lanes], axis=0)
            m_n = jnp.maximum(m_o, m_b)
            w_o = jnp.exp(m_o - m_n)
            w_b = jnp.exp(m_b - m_n)
            l_b = l_o * w_o + l_b * w_b
            pv_b = a_o * w_o + pv_b * w_b
            m_b = m_n
        last = stage == len(DILATED_PATTERNS) - 1
        if last:
            out = pv_b / l_b
        for c, ls in enumerate(lanes):
            sub = slice(c * rows, (c + 1) * rows)
            if last:
                o_ref[pl.ds(MAX_DIL * rows * blk + groups[c], rows, stride=MAX_DIL), :] = out[sub]
            else:
                m_s[q_rows, ls] = m_b[sub]
                l_s[q_rows, ls] = l_b[sub]
                acc_s[q_rows, ls] = pv_b[sub]

    for unit in units:
        finish(unit, *scores(unit))


def _prompt_attn(q, k, v):
    n_hp, t, _ = q.shape
    n_super = t // SUPER
    width = MAX_DIL * LANES
    cur = pl.BlockSpec((None, SUPER, LANES), lambda h, i: (h, i, 0))
    win = pl.BlockSpec((None, LANES, SUPER), lambda h, i: (h, 0, 0))
    win_shape = jax.ShapeDtypeStruct((n_hp, LANES, SUPER), F32)
    return pl.pallas_call(
        _prompt_attn_kernel,
        grid=(n_hp, n_super),
        in_specs=[cur, cur, cur],
        out_specs=[cur, win, win],
        out_shape=[jax.ShapeDtypeStruct(q.shape, F32), win_shape, win_shape],
        scratch_shapes=[
            pltpu.VMEM((BAND, width), F32),
            pltpu.VMEM((2 * BAND, width), F32),
            pltpu.VMEM((2 * BAND, width), F32),
            pltpu.VMEM((BAND, width), F32),
            pltpu.VMEM((BAND, width), F32),
            pltpu.VMEM((BAND, width), F32),
        ],
        compiler_params=pltpu.CompilerParams(dimension_semantics=("arbitrary", "arbitrary"),
                                             vmem_limit_bytes=VMEM_LIMIT),
        name="prompt_attn",
    )(q, k, v)


FF_CHUNK = 256


def _outffn_begin(x_ref, oa_ref, obc_ref, wout_ref, g2_ref):
    mix = jnp.concatenate([oa_ref[c] for c in range(oa_ref.shape[0])] + [obc_ref[...]], axis=1).astype(BF16)
    x1 = x_ref[...] + jnp.dot(mix, wout_ref[...], preferred_element_type=F32)
    return x1, _rms(x1, g2_ref[...]).astype(BF16)


def _ffn_chunks(h2, acc, wgu_ref, wdown_ref, chunks):
    for c in chunks:
        g = jnp.dot(h2, wgu_ref[:, c * FF_CHUNK:(c + 1) * FF_CHUNK], preferred_element_type=F32)
        up = jnp.dot(h2, wgu_ref[:, D_FF + c * FF_CHUNK:D_FF + (c + 1) * FF_CHUNK], preferred_element_type=F32)
        act = (jax.nn.silu(g) * up).astype(BF16)
        acc = acc + jnp.dot(act, wdown_ref[c * FF_CHUNK:(c + 1) * FF_CHUNK, :], preferred_element_type=F32)
    return acc


def _outffn_rows(x_ref, oa_ref, obc_ref, wout_ref, g2_ref, wgu_ref, wdown_ref, fg_ref, final):
    x1, h2 = _outffn_begin(x_ref, oa_ref, obc_ref, wout_ref, g2_ref)
    acc = _ffn_chunks(h2, x1, wgu_ref, wdown_ref, range(D_FF // FF_CHUNK))
    return _rms(acc, fg_ref[...]) if final else acc


def _outffn_kernel(x_ref, oa_ref, obc_ref, wout_ref, g2_ref, wgu_ref, wdown_ref, fg_ref, o_ref, *, final):
    o_ref[...] = _outffn_rows(x_ref, oa_ref, obc_ref, wout_ref, g2_ref, wgu_ref, wdown_ref, fg_ref, final)


def _outffn_specs(oa, layer, pw, tm, index):
    return [
        pl.BlockSpec((tm, D_MODEL), lambda *g: (index(*g), 0)),
        pl.BlockSpec((oa.shape[0], tm, oa.shape[2]), lambda *g: (0, index(*g), 0)),
        pl.BlockSpec((tm, D_B + D_C), lambda *g: (index(*g), 0)),
    ] + [_layer_spec(layer, pw[k].shape[1:]) for k in _FFN_PARAMS] + [_const_spec((1, D_MODEL))]


def _outffn(x, oa, obc, layer, pw, final_g, final, tm):
    t = x.shape[0]
    return pl.pallas_call(
        functools.partial(_outffn_kernel, final=final),
        grid=(t // tm,),
        in_specs=_outffn_specs(oa, layer, pw, tm, lambda i: i),
        out_specs=pl.BlockSpec((tm, D_MODEL), lambda i: (i, 0)),
        out_shape=jax.ShapeDtypeStruct((t, D_MODEL), F32),
        compiler_params=pltpu.CompilerParams(dimension_semantics=("parallel",), vmem_limit_bytes=VMEM_LIMIT),
        name="outffn",
    )(x, oa, obc, *[pw[k] for k in _FFN_PARAMS], final_g)


def _sample_proj_kernel(x_ref, g1_ref, win_ref, cos_ref, sin_ref, wpool_ref, pscale_ref, convw_ref, convb_ref,
                        wg_ref, bg_ref, lam_ref, spool_ref, sconv_ref, h0_ref,
                        q_ref, k_ref, v_ref, obc_ref, npool_ref, nconv_ref, nh_ref,
                        ubuf, xbuf):
    n = x_ref.shape[0]
    nb = n // SUBLANES
    proj = _project(x_ref[...], g1_ref[...], win_ref[...])
    cos = cos_ref[...]
    sin = sin_ref[...]
    for c in range(HEAD_SPLIT):
        sl = slice(c * D_A_BLK, (c + 1) * D_A_BLK)
        q_ref[c] = _rope_tile(proj[:, _Q0:_K0][:, sl], cos, sin)
        k_ref[c] = _rope_tile(proj[:, _K0:_V0][:, sl], cos, sin)
        v_ref[c] = proj[:, _V0:_U0][:, sl]

    u = proj[:, _U0:_X0]
    xr = proj[:, _X0:_G0]
    gt = proj[:, _G0:]
    ubuf[:, 0:POOL_MAX, :] = spool_ref[...]
    ubuf[:, POOL_MAX:, :] = u.reshape(nb, SUBLANES, D_B)
    xbuf[:, 0:SUBLANES, :] = sconv_ref[...]
    xbuf[:, SUBLANES:, :] = xr.reshape(nb, SUBLANES, D_C)

    lo = lax.broadcasted_iota(jnp.int32, (n, LANES), 1) < HEAD_DIM
    cnt0 = jnp.where(lo, 2.0, 4.0)
    cnt1 = jnp.where(lo, 8.0, 16.0)
    sums0 = _window_sums_shifted(lambda s: ubuf[:, pl.ds(POOL_MAX - s, SUBLANES), 0:LANES].reshape(n, LANES), 2, 4)
    sums1 = _window_sums_shifted(lambda s: ubuf[:, pl.ds(POOL_MAX - s, SUBLANES), LANES:D_B].reshape(n, LANES), 8, 16)
    o_b = _pool_map(sums0, sums1, u, cnt0, cnt1, wpool_ref[...], pscale_ref[...])

    a, b = _rglru_terms(lambda s: xbuf[:, pl.ds(SUBLANES - s, SUBLANES), :].reshape(n, D_C), convw_ref[...],
                        convb_ref[...], wg_ref[...], bg_ref[...], lam_ref[...])
    h = _scan_groups(a, b, h0_ref[...])
    obc_ref[:, 0:D_B] = o_b
    obc_ref[:, D_B:] = h * jax.nn.gelu(gt)
    npool_ref[...] = ubuf[:, SUBLANES:, :]
    nconv_ref[...] = xr.reshape(nb, SUBLANES, D_C)
    nh_ref[...] = h


def _sample_proj(x, layer, pw, cos, sin, spool, sconv, h0rows):
    n = x.shape[0]
    nb = n // SUBLANES
    shapes = [
        jax.ShapeDtypeStruct((HEAD_SPLIT, n, D_A_BLK), F32), jax.ShapeDtypeStruct((HEAD_SPLIT, n, D_A_BLK), F32),
        jax.ShapeDtypeStruct((HEAD_SPLIT, n, D_A_BLK), F32),
        jax.ShapeDtypeStruct((n, D_B + D_C), F32),
        jax.ShapeDtypeStruct((nb, POOL_MAX, D_B), F32),
        jax.ShapeDtypeStruct((nb, SUBLANES, D_C), F32),
        jax.ShapeDtypeStruct((n, D_C), F32),
    ]
    data = (spool, sconv, h0rows)
    in_specs = ([_const_spec(x.shape)] + [_layer_spec(layer, pw[k].shape[1:]) for k in _PROJ_PARAMS]
                + [_const_spec(cos.shape), _const_spec(sin.shape)]
                + [_layer_spec(layer, pw[k].shape[1:]) for k in _MIX_PARAMS] + [_const_spec(a.shape) for a in data])
    args = (x, *[pw[k] for k in _PROJ_PARAMS], cos, sin, *[pw[k] for k in _MIX_PARAMS], *data)
    return pl.pallas_call(
        _sample_proj_kernel,
        grid=(1,),
        in_specs=in_specs,
        out_specs=[_const_spec(s.shape) for s in shapes],
        out_shape=shapes,
        scratch_shapes=[
            pltpu.VMEM((nb, POOL_MAX + SUBLANES, D_B), F32),
            pltpu.VMEM((nb, 2 * SUBLANES, D_C), F32),
        ],
        compiler_params=pltpu.CompilerParams(dimension_semantics=("arbitrary",), vmem_limit_bytes=VMEM_LIMIT),
        name="sample_proj",
    )(*args)


def _sample_key_counts(w_buf, t_new, n_heads):
    k = np.arange(w_buf + LANES)[None, :]
    dist = w_buf + np.arange(t_new)[:, None] - k
    cnt = np.zeros(dist.shape, np.float32)
    for window, dil in DILATED_PATTERNS:
        cnt += ((dist >= 0) & (dist <= window) & (dist % dil == 0)).astype(np.float32)
    return np.tile(cnt, (n_heads, 1))


def _shift_window(c_ref, new_ref, s_ref):
    da, w_buf = c_ref.shape
    t_new = new_ref.shape[0]
    tail = lax.broadcasted_iota(jnp.int32, (da, LANES), 1) >= LANES - t_new
    rolled = pltpu.roll(c_ref[...], w_buf - t_new, axis=1)
    new_t = jnp.concatenate([jnp.zeros((LANES - t_new, da), F32), new_ref[...]], axis=0).T
    s_ref[:, 0:w_buf - LANES] = rolled[:, 0:w_buf - LANES]
    s_ref[:, w_buf - LANES:] = jnp.where(tail, new_t, rolled[:, w_buf - LANES:])


def _sample_attn_scores(ck_ref, q_ref, kn_ref):
    da = ck_ref.shape[0]
    t_new = q_ref.shape[0]
    qt = jnp.concatenate([q_ref[...]] * (da // HEAD_DIM), axis=0)
    row = lax.broadcasted_iota(jnp.int32, qt.shape, 0)
    lane = lax.broadcasted_iota(jnp.int32, qt.shape, 1)
    own = (row // t_new) == (lane // HEAD_DIM)
    qbd = jnp.where(own, qt, 0.0).astype(BF16)
    kn_pad = jnp.concatenate([kn_ref[...], jnp.zeros((LANES - t_new, da), F32)], axis=0).astype(BF16)
    return jnp.concatenate([
        jnp.dot(qbd, ck_ref[...].astype(BF16), preferred_element_type=F32),
        lax.dot_general(qbd, kn_pad, (((1,), (1,)), ((), ())), preferred_element_type=F32)], axis=1)


def _sample_attn_finish(s, cnt_ref, cv_ref, vn_ref, oa_ref):
    da, w_buf = cv_ref.shape
    t_new = vn_ref.shape[0]
    cv = cv_ref[...]
    vn = vn_ref[...]
    zpad = jnp.zeros((LANES - t_new, da), F32)

    cnt = cnt_ref[...]
    s = jnp.where(cnt > 0.0, s, NEG)
    m = jnp.max(s, axis=1, keepdims=True)
    p = jnp.exp(s - m) * cnt
    den = jnp.sum(p, axis=1, keepdims=True)
    pb = p.astype(BF16)
    vn_pad = jnp.concatenate([vn, zpad], axis=0).astype(BF16)
    o_t = lax.dot_general(cv.astype(BF16), pb[:, :w_buf], (((1,), (1,)), ((), ())), preferred_element_type=F32)
    o_c = jnp.concatenate([o_t, jnp.zeros((da, LANES - o_t.shape[1]), F32)], axis=1).T[:o_t.shape[1]]
    o_n = (o_c + jnp.dot(pb[:, w_buf:], vn_pad, preferred_element_type=F32)) / den
    out = jnp.zeros((t_new, da), F32)
    lane_o = lax.broadcasted_iota(jnp.int32, (t_new, da), 1) // HEAD_DIM
    for h in range(da // HEAD_DIM):
        out = jnp.where(lane_o == h, o_n[h * t_new:(h + 1) * t_new, :], out)
    oa_ref[...] = out


def _outffn_sample_attn_kernel(x_ref, oa_ref, obc_ref, wout_ref, g2_ref, wgu_ref, wdown_ref, fg_ref,
                               cnt_ref, ck_ref, cv_ref, q_ref, kn_ref, vn_ref, *rest, final, parts):
    o_ref, sk_ref, sv_ref, oas_ref, acc_scr, h2_scr = rest[-6:]

    for part, chunks in enumerate(parts):
        @pl.when(pl.program_id(1) == part)
        def _(part=part, chunks=chunks):
            scores = _sample_attn_scores(ck_ref, q_ref, kn_ref)
            _shift_window(ck_ref, kn_ref, sk_ref)
            _shift_window(cv_ref, vn_ref, sv_ref)
            if part == 0:
                acc, h2 = _outffn_begin(x_ref, oa_ref, obc_ref, wout_ref, g2_ref)
                h2_scr[...] = h2
            else:
                acc, h2 = acc_scr[...], h2_scr[...]
            acc = _ffn_chunks(h2, acc, wgu_ref, wdown_ref, chunks)
            if part == len(parts) - 1:
                o_ref[...] = _rms(acc, fg_ref[...]) if final else acc
            else:
                acc_scr[...] = acc
            _sample_attn_finish(scores, cnt_ref, cv_ref, vn_ref, oas_ref)


def _outffn_sample_attn(x, oa, obc, layer, pw, final_g, final, tm, cache_k, cache_v, q, kn, vn, win_bufs, counts):
    t = x.shape[0]
    n_tiles = t // tm
    depth, nb, split, da, w_buf = cache_k.shape
    t_new = q.shape[1] // nb
    per_tile = nb * split // n_tiles
    assert per_tile * n_tiles == nb * split
    parts = [tuple(int(c) for c in p) for p in np.array_split(np.arange(D_FF // FF_CHUNK), per_tile)[::-1]]

    def group(i, j):
        g = i * per_tile + j
        return g // split, g % split

    def cache_index(i, j):
        b, c = group(i, j)
        return (layer, b, c, 0, 0)

    def tok_index(i, j):
        b, c = group(i, j)
        return (c, b, 0)

    cache_spec = pl.BlockSpec((None, None, None, da, w_buf), cache_index)
    tok_spec = pl.BlockSpec((None, t_new, da), tok_index)
    in_specs = _outffn_specs(oa, layer, pw, tm, lambda i, j: i) + [
        _const_spec(counts.shape), cache_spec, cache_spec, tok_spec, tok_spec, tok_spec]
    args = [x, oa, obc, *[pw[k] for k in _FFN_PARAMS], final_g, counts, cache_k, cache_v, q, kn, vn]
    aliases = {}
    if win_bufs is not None:
        in_specs += [pl.BlockSpec(memory_space=pl.ANY)] * 2
        aliases = {len(args): 1, len(args) + 1: 2}
        args += list(win_bufs)
    win_shape = jax.ShapeDtypeStruct(cache_k.shape, F32)
    y, sk, sv, oa_s = pl.pallas_call(
        functools.partial(_outffn_sample_attn_kernel, final=final, parts=parts),
        grid=(n_tiles, per_tile),
        in_specs=in_specs,
        out_specs=[pl.BlockSpec((tm, D_MODEL), lambda i, j: (i, 0)), cache_spec, cache_spec, tok_spec],
        out_shape=[jax.ShapeDtypeStruct((t, D_MODEL), F32), win_shape, win_shape, jax.ShapeDtypeStruct(q.shape, F32)],
        input_output_aliases=aliases,
        scratch_shapes=[pltpu.VMEM((tm, D_MODEL), F32), pltpu.VMEM((tm, D_MODEL), BF16)],
        compiler_params=pltpu.CompilerParams(dimension_semantics=("arbitrary", "arbitrary"),
                                             vmem_limit_bytes=VMEM_LIMIT),
        name="outffn_sample_attn",
    )(*args)
    return y, (sk, sv), oa_s


def _rope_tables(pos):
    half = HEAD_DIM // 2
    inv = jnp.power(ROPE_THETA, -2.0 * jnp.arange(half, dtype=F32) / HEAD_DIM)
    ang = pos.astype(F32)[:, None] * inv[None, :]
    cos = jnp.tile(jnp.cos(ang), (1, LANES // half))
    sin = jnp.sin(ang)
    sin_signed = jnp.tile(jnp.concatenate([-sin, sin], axis=1), (1, LANES // HEAD_DIM))
    return cos, sin_signed


def _rope_tile_tables(seq):
    half = HEAD_DIM // 2
    inv = jnp.power(ROPE_THETA, -2.0 * jnp.arange(half, dtype=F32) / HEAD_DIM)
    inv = jnp.tile(inv, LANES // half)[None, :]
    sign = jnp.tile(jnp.concatenate([-jnp.ones(half, F32), jnp.ones(half, F32)]), LANES // HEAD_DIM)[None, :]
    ang_r = jnp.arange(PROJ_SUB, dtype=F32)[:, None] * inv
    ang_b = (jnp.arange(seq // PROJ_SUB, dtype=F32) * PROJ_SUB)[:, None] * inv
    cr, sr = jnp.cos(ang_r), jnp.sin(ang_r)
    return cr, sr, sign * cr, sign * sr, jnp.cos(ang_b), jnp.sin(ang_b)


def _block_diag(w):
    depth, n, c, d = w.shape
    eye = jnp.eye(n, dtype=w.dtype)
    return (eye[None, :, None, :, None] * w[:, :, :, None, :]).reshape(depth, n * c, n * d)


_PROJ_PARAMS = ("g1", "w_in")
_MIX_PARAMS = ("wpool", "pscale", "convw", "convb", "wg", "bg", "lam")
_FFN_PARAMS = ("w_out", "g2", "w_gu", "w_down")


def _stack_params(norm1_g, w_in, pool_w, pool_scale, conv_w, conv_b, gate_a_w, gate_a_b, gate_x_w, gate_x_b, lru_lambda,
                  w_out, norm2_g, w_gu, w_down):
    q_scale = jnp.where(jnp.arange(D_IN) < D_A, HEAD_DIM ** -0.5, 1.0).astype(F32)
    return dict(
        g1=norm1_g[:, None, :],
        w_in=(w_in * q_scale).astype(BF16),
        wpool=_block_diag(pool_w).astype(BF16),
        pscale=pool_scale[:, None, :],
        convw=conv_w,
        convb=conv_b[:, None, :],
        wg=jnp.concatenate([_block_diag(gate_a_w), _block_diag(gate_x_w)], axis=2).astype(BF16),
        bg=jnp.concatenate([gate_a_b, gate_x_b], axis=1)[:, None, :],
        lam=lru_lambda[:, None, :],
        w_out=w_out.astype(BF16),
        g2=norm2_g[:, None, :],
        w_gu=w_gu.astype(BF16),
        w_down=w_down.astype(BF16),
    )


def kernel(x_prompt, x_sample, cache_win_k, cache_win_v, state_pool, state_conv, state_rglru, norm1_g, w_in, pool_w,
           pool_scale, conv_w, conv_b, gate_a_w, gate_a_b, gate_x_w, gate_x_b, lru_lambda, w_out, norm2_g, w_gu,
           w_down, final_g):
    batch, seq, _ = x_prompt.shape
    nb, t_new, _ = x_sample.shape
    depth, _, w_buf = cache_win_k.shape[:3]
    assert batch == 1 and seq % SUPER == 0 and seq % PROJ_TILE == 0 and seq % FFN_TILE == 0
    assert t_new == SUBLANES and w_buf == W_MAX and SUPER == W_MAX

    rope_p = _rope_tile_tables(seq)
    cos_s, sin_s = _rope_tables(PAST_LEN + jnp.arange(t_new, dtype=jnp.int32))
    cos_s = jnp.tile(cos_s, (nb, 1))
    sin_s = jnp.tile(sin_s, (nb, 1))
    counts = jnp.asarray(_sample_key_counts(w_buf, t_new, D_A_BLK // HEAD_DIM))
    fg = final_g[None, :]

    ck = jnp.transpose(cache_win_k, (0, 1, 3, 4, 2)).reshape(depth, nb, HEAD_SPLIT, D_A_BLK, w_buf)
    cv = jnp.transpose(cache_win_v, (0, 1, 3, 4, 2)).reshape(depth, nb, HEAD_SPLIT, D_A_BLK, w_buf)
    spool = jnp.pad(state_pool, ((0, 0), (0, 0), (1, 0), (0, 0)))
    sconv = jnp.pad(state_conv, ((0, 0), (0, 0), (SUBLANES - (CONV_W - 1), 0), (0, 0)))
    h0rows = jnp.repeat(state_rglru, t_new, axis=1)

    hp = x_prompt.reshape(seq, D_MODEL)
    hs = x_sample.reshape(nb * t_new, D_MODEL)
    pw = _stack_params(norm1_g, w_in, pool_w, pool_scale, conv_w, conv_b, gate_a_w, gate_a_b, gate_x_w, gate_x_b,
                       lru_lambda, w_out, norm2_g, w_gu, w_down)
    win_bufs = None
    p_k, p_v, p_pool, p_conv, p_h, s_pool, s_conv, s_h = [], [], [], [], [], [], [], []
    for l in range(depth):
        last = l == depth - 1
        qs, kn, vn, obc_s, npool, nconv, nh = _sample_proj(hs, l, pw, cos_s, sin_s, spool[l], sconv[l], h0rows[l])
        q, k, v, obc, pst, cst, hst = _prompt_proj(hp, l, pw, rope_p, PROJ_TILE)
        oa, kt, vt = _prompt_attn(q, k, v)
        hp, win_bufs, oa_s = _outffn_sample_attn(hp, oa, obc, l, pw, fg, last, FFN_TILE, ck, cv, qs, kn, vn,
                                                 win_bufs, counts)
        p_k.append(kt)
        p_v.append(vt)
        p_pool.append(pst[None, 1:, :])
        p_conv.append(cst[None, SUBLANES - (CONV_W - 1):, :])
        p_h.append(hst[SUBLANES - 1:, :])
        hs = _outffn(hs, oa_s, obc_s, l, pw, fg, last, nb * t_new)
        s_pool.append(npool[:, 1:, :])
        s_conv.append(nconv[:, SUBLANES - (CONV_W - 1):, :])
        s_h.append(nh.reshape(nb, t_new, D_C)[:, t_new - 1, :])

    y_prompt = hp.reshape(batch, seq, D_MODEL)
    y_sample = hs.reshape(nb, t_new, D_MODEL)
    s_win_k = jnp.transpose(win_bufs[0].reshape(depth, nb, N_HEADS_A, HEAD_DIM, w_buf), (0, 1, 4, 2, 3))
    s_win_v = jnp.transpose(win_bufs[1].reshape(depth, nb, N_HEADS_A, HEAD_DIM, w_buf), (0, 1, 4, 2, 3))
    p_win_k = jnp.transpose(jnp.stack(p_k).reshape(depth, 1, N_HEADS_A, HEAD_DIM, W_MAX), (0, 1, 4, 2, 3))
    p_win_v = jnp.transpose(jnp.stack(p_v).reshape(depth, 1, N_HEADS_A, HEAD_DIM, W_MAX), (0, 1, 4, 2, 3))
    return (y_prompt, y_sample, p_win_k, p_win_v, jnp.stack(p_pool), jnp.stack(p_conv),
            jnp.stack(p_h), s_win_k, s_win_v, jnp.stack(s_pool), jnp.stack(s_conv), jnp.stack(s_h))
```

```python
import functools

import numpy as np
import jax
import jax.numpy as jnp
from jax import lax
from jax.experimental import pallas as pl
from jax.experimental.pallas import tpu as pltpu

F32 = jnp.float32
BF16 = jnp.bfloat16

D_MODEL = 1024
HEAD_DIM = 64
D_A = 384
N_HEADS_A = D_A // HEAD_DIM
D_B = 256
D_C = 384
D_IN = 3 * D_A + D_B + 2 * D_C
D_FF = 2816
POOL_MAX = 16
CONV_W = 4
RG_C = 8.0
EPS = 1e-6
ROPE_THETA = 10000.0
W_MAX = 2048
PAST_LEN = 16384
DILATED_PATTERNS = ((128, 1), (512, 4), (2048, 16))
BAND = 128
MAX_DIL = 16
SUPER = BAND * MAX_DIL

LANES = 128
SUBLANES = 8
NEG = -1e30
VMEM_LIMIT = 56 * 1024 * 1024

PROJ_TILE = 1024
PROJ_SUB = 256
PROJ_AHEAD = 2
FFN_TILE = 512
HEAD_SPLIT = 3
D_A_BLK = D_A // HEAD_SPLIT

_Q0, _K0, _V0, _U0, _X0, _G0 = 0, D_A, 2 * D_A, 3 * D_A, 3 * D_A + D_B, 3 * D_A + D_B + D_C


def _rms(x, g):
    return x * lax.rsqrt(jnp.mean(x * x, axis=-1, keepdims=True) + EPS) * g


def _rope_tile(t, cos, sin_signed):
    lane = lax.broadcasted_iota(jnp.int32, t.shape, 1)
    first = (lane % HEAD_DIM) < (HEAD_DIM // 2)
    swapped = jnp.where(first, pltpu.roll(t, LANES - HEAD_DIM // 2, axis=1), pltpu.roll(t, HEAD_DIM // 2, axis=1))
    return t * cos + swapped * sin_signed


def _log_sigmoid(x):
    return jnp.minimum(x, 0.0) - jnp.log1p(jnp.exp(-jnp.abs(x)))


def _project(x, g1, w_in):
    hn = _rms(x, g1)
    return jnp.dot(hn.astype(BF16), w_in, preferred_element_type=F32)


def _pool_map(sums0, sums1, u, cnt0, cnt1, wpool, pscale):
    lo = lax.broadcasted_iota(jnp.int32, cnt0.shape, 1) < HEAD_DIM
    mean0 = jnp.where(lo, sums0[0], sums0[1]) / cnt0
    mean1 = jnp.where(lo, sums1[0], sums1[1]) / cnt1
    pooled = jnp.concatenate([mean0, mean1], axis=1) - u
    return jnp.dot(pooled.astype(BF16), wpool, preferred_element_type=F32) * pscale


def _window_sums_shifted(shift, narrow, wide):
    s = shift(0)
    for i in range(1, narrow):
        s = s + shift(i)
    s_narrow = s
    for i in range(narrow, wide):
        s = s + shift(i)
    return s_narrow, s


def _window_sums_rolled(xe, halo):
    s2 = xe + pltpu.roll(xe, 1, axis=0)
    s4 = s2 + pltpu.roll(s2, 2, axis=0)
    s8 = s4 + pltpu.roll(s4, 4, axis=0)
    s16 = s8 + pltpu.roll(s8, 8, axis=0)
    return s2[halo:], s4[halo:], s8[halo:], s16[halo:]


def _rglru_terms(xshift, convw, convb, wg, bg, lam):
    xc = convb + (xshift(3) * convw[0:1] + xshift(2) * convw[1:2] + xshift(1) * convw[2:3] + xshift(0) * convw[3:4])
    g = jnp.dot(xc.astype(BF16), wg, preferred_element_type=F32) + bg
    r = jax.nn.sigmoid(g[:, :D_C])
    ig = jax.nn.sigmoid(g[:, D_C:])
    half_log_a = (0.5 * RG_C) * r * _log_sigmoid(lam)
    a = jnp.exp(2.0 * half_log_a)
    z = -jnp.tanh(half_log_a)
    root = jnp.where(z > 0.0, z * lax.rsqrt(z), 0.0)
    b = (1.0 + a) * root * ig * xc
    return a, b


def _compose_step(a, b, shift):
    b = a * pltpu.roll(b, shift, axis=0) + b
    a = a * pltpu.roll(a, shift, axis=0)
    return a, b


def _scan_groups(a, b, h0_rows):
    row = lax.broadcasted_iota(jnp.int32, a.shape, 0) % SUBLANES
    for s in (1, 2, 4):
        keep = row >= s
        a_sh = jnp.where(keep, pltpu.roll(a, s, axis=0), 1.0)
        b_sh = jnp.where(keep, pltpu.roll(b, s, axis=0), 0.0)
        b = a * b_sh + b
        a = a * a_sh
    return a * h0_rows + b


def _scan_tile(a, b, h_last):
    n, c = a.shape
    a = jnp.concatenate([jnp.ones((SUBLANES, c), F32), a], axis=0)
    b = jnp.concatenate([jnp.zeros((SUBLANES, c), F32), b], axis=0)
    for s in (1, 2, 4):
        a, b = _compose_step(a, b, s)
    h = jnp.broadcast_to(h_last, (SUBLANES, c))
    out = []
    for j in range(1, n // SUBLANES + 1):
        rows = slice(j * SUBLANES, (j + 1) * SUBLANES)
        h = a[rows] * h + b[rows]
        out.append(h)
    return jnp.concatenate(out, axis=0)


def _prompt_proj_kernel(x_ref, g1_ref, win_ref, cr_ref, sr_ref, crs_ref, srs_ref, cb_ref, sb_ref,
                        wpool_ref, pscale_ref, convw_ref, convb_ref, wg_ref, bg_ref, lam_ref,
                        q_ref, k_ref, v_ref, obc_ref, pst_ref, cst_ref, hst_ref,
                        ubuf, xbuf, hcar):
    i = pl.program_id(0)
    tm = x_ref.shape[0]

    @pl.when(i == 0)
    def _():
        ubuf[0:POOL_MAX, :] = jnp.zeros((POOL_MAX, D_B), F32)
        xbuf[0:SUBLANES, :] = jnp.zeros((SUBLANES, D_C), F32)
        hcar[...] = jnp.zeros((SUBLANES, D_C), F32)

    def project(r0):
        rows = slice(r0, r0 + PROJ_SUB)
        proj = _project(x_ref[rows, :], g1_ref[...], win_ref[...])
        sub = i * (tm // PROJ_SUB) + r0 // PROJ_SUB
        cb = cb_ref[pl.ds(sub, 1), :]
        sb = sb_ref[pl.ds(sub, 1), :]
        cos = cb * cr_ref[...] - sb * sr_ref[...]
        sin = sb * crs_ref[...] + cb * srs_ref[...]
        for hp in range(D_A // LANES):
            sl = slice(hp * LANES, (hp + 1) * LANES)
            q_ref[hp, rows, :] = _rope_tile(proj[:, _Q0:_K0][:, sl], cos, sin)
            k_ref[hp, rows, :] = _rope_tile(proj[:, _K0:_V0][:, sl], cos, sin)
            v_ref[hp, rows, :] = proj[:, _V0:_U0][:, sl]
        u = proj[:, _U0:_X0]
        xr = proj[:, _X0:_G0]
        ubuf[POOL_MAX + r0:POOL_MAX + r0 + PROJ_SUB, :] = u
        xbuf[SUBLANES + r0:SUBLANES + r0 + PROJ_SUB, :] = xr
        return u, xr, proj[:, _G0:]

    def mix(r0, u, gt, h_last):
        rows = slice(r0, r0 + PROJ_SUB)
        pos1 = (i * tm + r0 + 1 + lax.broadcasted_iota(jnp.int32, (PROJ_SUB, LANES), 0)).astype(F32)
        lo = lax.broadcasted_iota(jnp.int32, (PROJ_SUB, LANES), 1) < HEAD_DIM
        cnt0 = jnp.minimum(pos1, jnp.where(lo, 2.0, 4.0))
        cnt1 = jnp.minimum(pos1, jnp.where(lo, 8.0, 16.0))
        ext = slice(r0, r0 + PROJ_SUB + POOL_MAX)
        s2, s4, _, _ = _window_sums_rolled(ubuf[ext, 0:LANES], POOL_MAX)
        _, _, s8, s16 = _window_sums_rolled(ubuf[ext, LANES:D_B], POOL_MAX)
        o_b = _pool_map((s2, s4), (s8, s16), u, cnt0, cnt1, wpool_ref[...], pscale_ref[...])

        a, b = _rglru_terms(lambda s: xbuf[pl.ds(SUBLANES + r0 - s, PROJ_SUB), :], convw_ref[...],
                            convb_ref[...], wg_ref[...], bg_ref[...], lam_ref[...])
        h = _scan_tile(a, b, h_last)
        obc_ref[rows, 0:D_B] = o_b
        obc_ref[rows, D_B:] = h * jax.nn.gelu(gt)
        return h

    h_last = hcar[SUBLANES - 1:SUBLANES, :]
    starts = list(range(0, tm, PROJ_SUB))
    pending = [project(r0) for r0 in starts[:PROJ_AHEAD]]
    for k, r0 in enumerate(starts):
        u, xr, gt = pending.pop(0)
        if k + PROJ_AHEAD < len(starts):
            pending.append(project(starts[k + PROJ_AHEAD]))
        h = mix(r0, u, gt, h_last)
        h_last = h[PROJ_SUB - 1:, :]

    ubuf[0:POOL_MAX, :] = u[PROJ_SUB - POOL_MAX:, :]
    xbuf[0:SUBLANES, :] = xr[PROJ_SUB - SUBLANES:, :]
    hcar[...] = h[PROJ_SUB - SUBLANES:, :]
    pst_ref[...] = u[PROJ_SUB - POOL_MAX:, :]
    cst_ref[...] = xr[PROJ_SUB - SUBLANES:, :]
    hst_ref[...] = h[PROJ_SUB - SUBLANES:, :]


def _const_spec(shape):
    return pl.BlockSpec(shape, lambda *_: (0,) * len(shape))


def _layer_spec(layer, shape):
    return pl.BlockSpec((None,) + tuple(shape), lambda *_: (layer,) + (0,) * len(shape),
                        pipeline_mode=pl.Buffered(1))


def _prompt_proj(x, layer, pw, rope, tm):
    t = x.shape[0]
    n_hp = D_A // LANES
    qkv_shape = jax.ShapeDtypeStruct((n_hp, t, LANES), F32)
    qkv_spec = pl.BlockSpec((n_hp, tm, LANES), lambda i: (0, i, 0))
    return pl.pallas_call(
        _prompt_proj_kernel,
        grid=(t // tm,),
        in_specs=([pl.BlockSpec((tm, D_MODEL), lambda i: (i, 0))]
                  + [_layer_spec(layer, pw[k].shape[1:]) for k in _PROJ_PARAMS]
                  + [_const_spec(r.shape) for r in rope]
                  + [_layer_spec(layer, pw[k].shape[1:]) for k in _MIX_PARAMS]),
        out_specs=[
            qkv_spec, qkv_spec, qkv_spec,
            pl.BlockSpec((tm, D_B + D_C), lambda i: (i, 0)),
            _const_spec((POOL_MAX, D_B)),
            _const_spec((SUBLANES, D_C)),
            _const_spec((SUBLANES, D_C)),
        ],
        out_shape=[
            qkv_shape, qkv_shape, qkv_shape,
            jax.ShapeDtypeStruct((t, D_B + D_C), F32),
            jax.ShapeDtypeStruct((POOL_MAX, D_B), F32),
            jax.ShapeDtypeStruct((SUBLANES, D_C), F32),
            jax.ShapeDtypeStruct((SUBLANES, D_C), F32),
        ],
        scratch_shapes=[
            pltpu.VMEM((POOL_MAX + tm, D_B), F32),
            pltpu.VMEM((SUBLANES + tm, D_C), F32),
            pltpu.VMEM((SUBLANES, D_C), F32),
        ],
        compiler_params=pltpu.CompilerParams(dimension_semantics=("arbitrary",), vmem_limit_bytes=VMEM_LIMIT),
        name="prompt_proj",
    )(x, *[pw[k] for k in _PROJ_PARAMS], *rope, *[pw[k] for k in _MIX_PARAMS])


def _band_bias(dil):
    slabs = MAX_DIL // dil
    rows = BAND // slabs
    row = lax.broadcasted_iota(jnp.int32, (BAND, 2 * BAND), 0)
    col = lax.broadcasted_iota(jnp.int32, (BAND, 2 * BAND), 1)
    qi = slabs * (row % rows) + row // rows
    ki = slabs * (col % (2 * rows)) + col // (2 * rows)
    valid = (ki >= qi) & (ki <= qi + BAND)
    bias = jnp.where(valid, 0.0, NEG)
    bias_first = jnp.where(valid & (ki >= BAND), 0.0, NEG)
    return bias, bias_first


def _prompt_attn_kernel(q_ref, k_ref, v_ref, o_ref, kt_ref, vt_ref, q16, kcat, vcat, acc_s, m_s, l_s):
    i = pl.program_id(1)

    @pl.when(i == pl.num_programs(1) - 1)
    def _():
        kt_ref[...] = k_ref[...].T
        vt_ref[...] = v_ref[...].T

    @pl.when(i == 0)
    def _():
        kcat[0:BAND, :] = jnp.zeros((BAND, MAX_DIL * LANES), F32)
        vcat[0:BAND, :] = jnp.zeros((BAND, MAX_DIL * LANES), F32)

    @pl.when(i > 0)
    def _():
        kcat[0:BAND, :] = kcat[BAND:, :]
        vcat[0:BAND, :] = vcat[BAND:, :]

    for r in range(MAX_DIL):
        ls = slice(r * LANES, (r + 1) * LANES)
        q16[:, ls] = q_ref[pl.ds(r, BAND, stride=MAX_DIL), :]
        kcat[BAND:, ls] = k_ref[pl.ds(r, BAND, stride=MAX_DIL), :]
        vcat[BAND:, ls] = v_ref[pl.ds(r, BAND, stride=MAX_DIL), :]

    head0 = lax.broadcasted_iota(jnp.int32, (BAND, LANES), 1) < HEAD_DIM
    ones_cols = jnp.ones((2 * BAND, LANES), BF16)

    units = []
    for stage, (_, dil) in enumerate(DILATED_PATTERNS):
        slabs = MAX_DIL // dil
        rows = BAND // slabs
        bias, bias_first = _band_bias(dil)
        bias_blk0 = jnp.where(i == 0, bias_first, bias)
        for blk in range(slabs):
            for res in range(dil):
                units.append((stage, slabs, rows, blk, [res + dil * c for c in range(slabs)],
                              bias_blk0 if blk == 0 else bias))

    def scores(unit):
        _, _, rows, blk, groups, _ = unit
        q_rows = slice(rows * blk, rows * (blk + 1))
        k_rows = slice(BAND + rows * (blk - 1), BAND + rows * (blk + 1))
        lanes = [slice(g * LANES, (g + 1) * LANES) for g in groups]
        qb = jnp.concatenate([q16[q_rows, ls] for ls in lanes], axis=0)
        kk = jnp.concatenate([kcat[k_rows, ls] for ls in lanes], axis=0).astype(BF16)
        vv = jnp.concatenate([vcat[k_rows, ls] for ls in lanes], axis=0).astype(BF16)
        q2 = jnp.concatenate([jnp.where(head0, qb, 0.0), jnp.where(head0, 0.0, qb)], axis=0).astype(BF16)
        s2 = lax.dot_general(q2, kk, (((1,), (1,)), ((), ())), preferred_element_type=F32)
        return s2, jnp.concatenate([vv, ones_cols], axis=1)

    def finish(unit, s2, vv1):
        stage, _, rows, blk, groups, b_u = unit
        q_rows = slice(rows * blk, rows * (blk + 1))
        lanes = [slice(g * LANES, (g + 1) * LANES) for g in groups]
        ms, ps = [], []
        for hh in range(2):
            s = s2[hh * BAND:(hh + 1) * BAND] + b_u
            m = jnp.max(s, axis=1, keepdims=True)
            ms.append(m)
            ps.append(jnp.exp((s - m).astype(BF16)))
        pv2 = jnp.dot(jnp.concatenate(ps, axis=0), vv1, preferred_element_type=F32)
        parts = [(ms[hh], pv2[hh * BAND:(hh + 1) * BAND, LANES:], pv2[hh * BAND:(hh + 1) * BAND, :LANES])
                 for hh in range(2)]
        m_b = jnp.where(head0, parts[0][0], parts[1][0])
        l_b = jnp.where(head0, parts[0][1], parts[1][1])
        pv_b = jnp.where(head0, parts[0][2], parts[1][2])
        if stage > 0:
            m_o = jnp.concatenate([m_s[q_rows, ls] for ls in lanes], axis=0)
            l_o = jnp.concatenate([l_s[q_rows, ls] for ls in lanes], axis=0)
            a_o = jnp.concatenate([acc_s[q_rows, ls] for ls in lanes], axis=0)
            m_n = jnp.maximum(m_o, m_b)
            w_o = jnp.exp(m_o - m_n)
            w_b = jnp.exp(m_b - m_n)
            l_b = l_o * w_o + l_b * w_b
            pv_b = a_o * w_o + pv_b * w_b
            m_b = m_n
        last = stage == len(DILATED_PATTERNS) - 1
        if last:
            out = pv_b / l_b
        for c, ls in enumerate(lanes):
            sub = slice(c * rows, (c + 1) * rows)
            if last:
                o_ref[pl.ds(MAX_DIL * rows * blk + groups[c], rows, stride=MAX_DIL), :] = out[sub]
            else:
                m_s[q_rows, ls] = m_b[sub]
                l_s[q_rows, ls] = l_b[sub]
                acc_s[q_rows, ls] = pv_b[sub]

    for unit in units:
        finish(unit, *scores(unit))


def _prompt_attn(q, k, v):
    n_hp, t, _ = q.shape
    n_super = t // SUPER
    width = MAX_DIL * LANES
    cur = pl.BlockSpec((None, SUPER, LANES), lambda h, i: (h, i, 0))
    win = pl.BlockSpec((None, LANES, SUPER), lambda h, i: (h, 0, 0))
    win_shape = jax.ShapeDtypeStruct((n_hp, LANES, SUPER), F32)
    return pl.pallas_call(
        _prompt_attn_kernel,
        grid=(n_hp, n_super),
        in_specs=[cur, cur, cur],
        out_specs=[cur, win, win],
        out_shape=[jax.ShapeDtypeStruct(q.shape, F32), win_shape, win_shape],
        scratch_shapes=[
            pltpu.VMEM((BAND, width), F32),
            pltpu.VMEM((2 * BAND, width), F32),
            pltpu.VMEM((2 * BAND, width), F32),
            pltpu.VMEM((BAND, width), F32),
            pltpu.VMEM((BAND, width), F32),
            pltpu.VMEM((BAND, width), F32),
        ],
        compiler_params=pltpu.CompilerParams(dimension_semantics=("arbitrary", "arbitrary"),
                                             vmem_limit_bytes=VMEM_LIMIT),
        name="prompt_attn",
    )(q, k, v)


FF_CHUNK = 256


def _outffn_begin(x_ref, oa_ref, obc_ref, wout_ref, g2_ref):
    mix = jnp.concatenate([oa_ref[c] for c in range(oa_ref.shape[0])] + [obc_ref[...]], axis=1).astype(BF16)
    x1 = x_ref[...] + jnp.dot(mix, wout_ref[...], preferred_element_type=F32)
    return x1, _rms(x1, g2_ref[...]).astype(BF16)


def _ffn_chunks(h2, acc, wgu_ref, wdown_ref, chunks):
    for c in chunks:
        g = jnp.dot(h2, wgu_ref[:, c * FF_CHUNK:(c + 1) * FF_CHUNK], preferred_element_type=F32)
        up = jnp.dot(h2, wgu_ref[:, D_FF + c * FF_CHUNK:D_FF + (c + 1) * FF_CHUNK], preferred_element_type=F32)
        act = (jax.nn.silu(g) * up).astype(BF16)
        acc = acc + jnp.dot(act, wdown_ref[c * FF_CHUNK:(c + 1) * FF_CHUNK, :], preferred_element_type=F32)
    return acc


def _outffn_rows(x_ref, oa_ref, obc_ref, wout_ref, g2_ref, wgu_ref, wdown_ref, fg_ref, final):
    x1, h2 = _outffn_begin(x_ref, oa_ref, obc_ref, wout_ref, g2_ref)
    acc = _ffn_chunks(h2, x1, wgu_ref, wdown_ref, range(D_FF // FF_CHUNK))
    return _rms(acc, fg_ref[...]) if final else acc


def _outffn_kernel(x_ref, oa_ref, obc_ref, wout_ref, g2_ref, wgu_ref, wdown_ref, fg_ref, o_ref, *, final):
    o_ref[...] = _outffn_rows(x_ref, oa_ref, obc_ref, wout_ref, g2_ref, wgu_ref, wdown_ref, fg_ref, final)


def _outffn_specs(oa, layer, pw, tm, index):
    return [
        pl.BlockSpec((tm, D_MODEL), lambda *g: (index(*g), 0)),
        pl.BlockSpec((oa.shape[0], tm, oa.shape[2]), lambda *g: (0, index(*g), 0)),
        pl.BlockSpec((tm, D_B + D_C), lambda *g: (index(*g), 0)),
    ] + [_layer_spec(layer, pw[k].shape[1:]) for k in _FFN_PARAMS] + [_const_spec((1, D_MODEL))]


def _outffn(x, oa, obc, layer, pw, final_g, final, tm):
    t = x.shape[0]
    return pl.pallas_call(
        functools.partial(_outffn_kernel, final=final),
        grid=(t // tm,),
        in_specs=_outffn_specs(oa, layer, pw, tm, lambda i: i),
        out_specs=pl.BlockSpec((tm, D_MODEL), lambda i: (i, 0)),
        out_shape=jax.ShapeDtypeStruct((t, D_MODEL), F32),
        compiler_params=pltpu.CompilerParams(dimension_semantics=("parallel",), vmem_limit_bytes=VMEM_LIMIT),
        name="outffn",
    )(x, oa, obc, *[pw[k] for k in _FFN_PARAMS], final_g)


def _sample_proj_kernel(x_ref, g1_ref, win_ref, cos_ref, sin_ref, wpool_ref, pscale_ref, convw_ref, convb_ref,
                        wg_ref, bg_ref, lam_ref, spool_ref, sconv_ref, h0_ref,
                        q_ref, k_ref, v_ref, obc_ref, npool_ref, nconv_ref, nh_ref,
                        ubuf, xbuf):
    n = x_ref.shape[0]
    nb = n // SUBLANES
    proj = _project(x_ref[...], g1_ref[...], win_ref[...])
    cos = cos_ref[...]
    sin = sin_ref[...]
    for c in range(HEAD_SPLIT):
        sl = slice(c * D_A_BLK, (c + 1) * D_A_BLK)
        q_ref[c] = _rope_tile(proj[:, _Q0:_K0][:, sl], cos, sin)
        k_ref[c] = _rope_tile(proj[:, _K0:_V0][:, sl], cos, sin)
        v_ref[c] = proj[:, _V0:_U0][:, sl]

    u = proj[:, _U0:_X0]
    xr = proj[:, _X0:_G0]
    gt = proj[:, _G0:]
    ubuf[:, 0:POOL_MAX, :] = spool_ref[...]
    ubuf[:, POOL_MAX:, :] = u.reshape(nb, SUBLANES, D_B)
    xbuf[:, 0:SUBLANES, :] = sconv_ref[...]
    xbuf[:, SUBLANES:, :] = xr.reshape(nb, SUBLANES, D_C)

    lo = lax.broadcasted_iota(jnp.int32, (n, LANES), 1) < HEAD_DIM
    cnt0 = jnp.where(lo, 2.0, 4.0)
    cnt1 = jnp.where(lo, 8.0, 16.0)
    sums0 = _window_sums_shifted(lambda s: ubuf[:, pl.ds(POOL_MAX - s, SUBLANES), 0:LANES].reshape(n, LANES), 2, 4)
    sums1 = _window_sums_shifted(lambda s: ubuf[:, pl.ds(POOL_MAX - s, SUBLANES), LANES:D_B].reshape(n, LANES), 8, 16)
    o_b = _pool_map(sums0, sums1, u, cnt0, cnt1, wpool_ref[...], pscale_ref[...])

    a, b = _rglru_terms(lambda s: xbuf[:, pl.ds(SUBLANES - s, SUBLANES), :].reshape(n, D_C), convw_ref[...],
                        convb_ref[...], wg_ref[...], bg_ref[...], lam_ref[...])
    h = _scan_groups(a, b, h0_ref[...])
    obc_ref[:, 0:D_B] = o_b
    obc_ref[:, D_B:] = h * jax.nn.gelu(gt)
    npool_ref[...] = ubuf[:, SUBLANES:, :]
    nconv_ref[...] = xr.reshape(nb, SUBLANES, D_C)
    nh_ref[...] = h


def _sample_proj(x, layer, pw, cos, sin, spool, sconv, h0rows):
    n = x.shape[0]
    nb = n // SUBLANES
    shapes = [
        jax.ShapeDtypeStruct((HEAD_SPLIT, n, D_A_BLK), F32), jax.ShapeDtypeStruct((HEAD_SPLIT, n, D_A_BLK), F32),
        jax.ShapeDtypeStruct((HEAD_SPLIT, n, D_A_BLK), F32),
        jax.ShapeDtypeStruct((n, D_B + D_C), F32),
        jax.ShapeDtypeStruct((nb, POOL_MAX, D_B), F32),
        jax.ShapeDtypeStruct((nb, SUBLANES, D_C), F32),
        jax.ShapeDtypeStruct((n, D_C), F32),
    ]
    data = (spool, sconv, h0rows)
    in_specs = ([_const_spec(x.shape)] + [_layer_spec(layer, pw[k].shape[1:]) for k in _PROJ_PARAMS]
                + [_const_spec(cos.shape), _const_spec(sin.shape)]
                + [_layer_spec(layer, pw[k].shape[1:]) for k in _MIX_PARAMS] + [_const_spec(a.shape) for a in data])
    args = (x, *[pw[k] for k in _PROJ_PARAMS], cos, sin, *[pw[k] for k in _MIX_PARAMS], *data)
    return pl.pallas_call(
        _sample_proj_kernel,
        grid=(1,),
        in_specs=in_specs,
        out_specs=[_const_spec(s.shape) for s in shapes],
        out_shape=shapes,
        scratch_shapes=[
            pltpu.VMEM((nb, POOL_MAX + SUBLANES, D_B), F32),
            pltpu.VMEM((nb, 2 * SUBLANES, D_C), F32),
        ],
        compiler_params=pltpu.CompilerParams(dimension_semantics=("arbitrary",), vmem_limit_bytes=VMEM_LIMIT),
        name="sample_proj",
    )(*args)


def _sample_key_counts(w_buf, t_new, n_heads):
    k = np.arange(w_buf + LANES)[None, :]
    dist = w_buf + np.arange(t_new)[:, None] - k
    cnt = np.zeros(dist.shape, np.float32)
    for window, dil in DILATED_PATTERNS:
        cnt += ((dist >= 0) & (dist <= window) & (dist % dil == 0)).astype(np.float32)
    return np.tile(cnt, (n_heads, 1))


def _shift_window(c_ref, new_ref, s_ref):
    da, w_buf = c_ref.shape
    t_new = new_ref.shape[0]
    tail = lax.broadcasted_iota(jnp.int32, (da, LANES), 1) >= LANES - t_new
    rolled = pltpu.roll(c_ref[...], w_buf - t_new, axis=1)
    new_t = jnp.concatenate([jnp.zeros((LANES - t_new, da), F32), new_ref[...]], axis=0).T
    s_ref[:, 0:w_buf - LANES] = rolled[:, 0:w_buf - LANES]
    s_ref[:, w_buf - LANES:] = jnp.where(tail, new_t, rolled[:, w_buf - LANES:])


def _sample_attn_scores(ck_ref, q_ref, kn_ref):
    da = ck_ref.shape[0]
    t_new = q_ref.shape[0]
    qt = jnp.concatenate([q_ref[...]] * (da // HEAD_DIM), axis=0)
    row = lax.broadcasted_iota(jnp.int32, qt.shape, 0)
    lane = lax.broadcasted_iota(jnp.int32, qt.shape, 1)
    own = (row // t_new) == (lane // HEAD_DIM)
    qbd = jnp.where(own, qt, 0.0).astype(BF16)
    kn_pad = jnp.concatenate([kn_ref[...], jnp.zeros((LANES - t_new, da), F32)], axis=0).astype(BF16)
    return jnp.concatenate([
        jnp.dot(qbd, ck_ref[...].astype(BF16), preferred_element_type=F32),
        lax.dot_general(qbd, kn_pad, (((1,), (1,)), ((), ())), preferred_element_type=F32)], axis=1)


def _sample_attn_finish(s, cnt_ref, cv_ref, vn_ref, oa_ref):
    da, w_buf = cv_ref.shape
    t_new = vn_ref.shape[0]
    cv = cv_ref[...]
    vn = vn_ref[...]
    zpad = jnp.zeros((LANES - t_new, da), F32)

    cnt = cnt_ref[...]
    s = jnp.where(cnt > 0.0, s, NEG)
    m = jnp.max(s, axis=1, keepdims=True)
    p = jnp.exp(s - m) * cnt
    den = jnp.sum(p, axis=1, keepdims=True)
    pb = p.astype(BF16)
    vn_pad = jnp.concatenate([vn, zpad], axis=0).astype(BF16)
    o_t = lax.dot_general(cv.astype(BF16), pb[:, :w_buf], (((1,), (1,)), ((), ())), preferred_element_type=F32)
    o_c = jnp.concatenate([o_t, jnp.zeros((da, LANES - o_t.shape[1]), F32)], axis=1).T[:o_t.shape[1]]
    o_n = (o_c + jnp.dot(pb[:, w_buf:], vn_pad, preferred_element_type=F32)) / den
    out = jnp.zeros((t_new, da), F32)
    lane_o = lax.broadcasted_iota(jnp.int32, (t_new, da), 1) // HEAD_DIM
    for h in range(da // HEAD_DIM):
        out = jnp.where(lane_o == h, o_n[h * t_new:(h + 1) * t_new, :], out)
    oa_ref[...] = out


def _outffn_sample_attn_kernel(x_ref, oa_ref, obc_ref, wout_ref, g2_ref, wgu_ref, wdown_ref, fg_ref,
                               cnt_ref, ck_ref, cv_ref, q_ref, kn_ref, vn_ref, *rest, final, parts):
    o_ref, sk_ref, sv_ref, oas_ref, acc_scr, h2_scr = rest[-6:]

    for part, chunks in enumerate(parts):
        @pl.when(pl.program_id(1) == part)
        def _(part=part, chunks=chunks):
            scores = _sample_attn_scores(ck_ref, q_ref, kn_ref)
            _shift_window(ck_ref, kn_ref, sk_ref)
            _shift_window(cv_ref, vn_ref, sv_ref)
            if part == 0:
                acc, h2 = _outffn_begin(x_ref, oa_ref, obc_ref, wout_ref, g2_ref)
                h2_scr[...] = h2
            else:
                acc, h2 = acc_scr[...], h2_scr[...]
            acc = _ffn_chunks(h2, acc, wgu_ref, wdown_ref, chunks[:-1])
            _sample_attn_finish(scores, cnt_ref, cv_ref, vn_ref, oas_ref)
            acc = _ffn_chunks(h2, acc, wgu_ref, wdown_ref, chunks[-1:])
            if part == len(parts) - 1:
                o_ref[...] = _rms(acc, fg_ref[...]) if final else acc
            else:
                acc_scr[...] = acc


def _outffn_sample_attn(x, oa, obc, layer, pw, final_g, final, tm, cache_k, cache_v, q, kn, vn, win_bufs, counts):
    t = x.shape[0]
    n_tiles = t // tm
    depth, nb, split, da, w_buf = cache_k.shape
    t_new = q.shape[1] // nb
    per_tile = nb * split // n_tiles
    assert per_tile * n_tiles == nb * split
    parts = [tuple(int(c) for c in p) for p in np.array_split(np.arange(D_FF // FF_CHUNK), per_tile)[::-1]]

    def group(i, j):
        g = i * per_tile + j
        return g // split, g % split

    def cache_index(i, j):
        b, c = group(i, j)
        return (layer, b, c, 0, 0)

    def tok_index(i, j):
        b, c = group(i, j)
        return (c, b, 0)

    cache_spec = pl.BlockSpec((None, None, None, da, w_buf), cache_index)
    tok_spec = pl.BlockSpec((None, t_new, da), tok_index)
    in_specs = _outffn_specs(oa, layer, pw, tm, lambda i, j: i) + [
        _const_spec(counts.shape), cache_spec, cache_spec, tok_spec, tok_spec, tok_spec]
    args = [x, oa, obc, *[pw[k] for k in _FFN_PARAMS], final_g, counts, cache_k, cache_v, q, kn, vn]
    aliases = {}
    if win_bufs is not None:
        in_specs += [pl.BlockSpec(memory_space=pl.ANY)] * 2
        aliases = {len(args): 1, len(args) + 1: 2}
        args += list(win_bufs)
    win_shape = jax.ShapeDtypeStruct(cache_k.shape, F32)
    y, sk, sv, oa_s = pl.pallas_call(
        functools.partial(_outffn_sample_attn_kernel, final=final, parts=parts),
        grid=(n_tiles, per_tile),
        in_specs=in_specs,
        out_specs=[pl.BlockSpec((tm, D_MODEL), lambda i, j: (i, 0)), cache_spec, cache_spec, tok_spec],
        out_shape=[jax.ShapeDtypeStruct((t, D_MODEL), F32), win_shape, win_shape, jax.ShapeDtypeStruct(q.shape, F32)],
        input_output_aliases=aliases,
        scratch_shapes=[pltpu.VMEM((tm, D_MODEL), F32), pltpu.VMEM((tm, D_MODEL), BF16)],
        compiler_params=pltpu.CompilerParams(dimension_semantics=("arbitrary", "arbitrary"),
                                             vmem_limit_bytes=VMEM_LIMIT),
        name="outffn_sample_attn",
    )(*args)
    return y, (sk, sv), oa_s


def _rope_tables(pos):
    half = HEAD_DIM // 2
    inv = jnp.power(ROPE_THETA, -2.0 * jnp.arange(half, dtype=F32) / HEAD_DIM)
    ang = pos.astype(F32)[:, None] * inv[None, :]
    cos = jnp.tile(jnp.cos(ang), (1, LANES // half))
    sin = jnp.sin(ang)
    sin_signed = jnp.tile(jnp.concatenate([-sin, sin], axis=1), (1, LANES // HEAD_DIM))
    return cos, sin_signed


def _rope_tile_tables(seq):
    half = HEAD_DIM // 2
    inv = jnp.power(ROPE_THETA, -2.0 * jnp.arange(half, dtype=F32) / HEAD_DIM)
    inv = jnp.tile(inv, LANES // half)[None, :]
    sign = jnp.tile(jnp.concatenate([-jnp.ones(half, F32), jnp.ones(half, F32)]), LANES // HEAD_DIM)[None, :]
    ang_r = jnp.arange(PROJ_SUB, dtype=F32)[:, None] * inv
    ang_b = (jnp.arange(seq // PROJ_SUB, dtype=F32) * PROJ_SUB)[:, None] * inv
    cr, sr = jnp.cos(ang_r), jnp.sin(ang_r)
    return cr, sr, sign * cr, sign * sr, jnp.cos(ang_b), jnp.sin(ang_b)


def _block_diag(w):
    depth, n, c, d = w.shape
    eye = jnp.eye(n, dtype=w.dtype)
    return (eye[None, :, None, :, None] * w[:, :, :, None, :]).reshape(depth, n * c, n * d)


_PROJ_PARAMS = ("g1", "w_in")
_MIX_PARAMS = ("wpool", "pscale", "convw", "convb", "wg", "bg", "lam")
_FFN_PARAMS = ("w_out", "g2", "w_gu", "w_down")


def _stack_params(norm1_g, w_in, pool_w, pool_scale, conv_w, conv_b, gate_a_w, gate_a_b, gate_x_w, gate_x_b, lru_lambda,
                  w_out, norm2_g, w_gu, w_down):
    q_scale = jnp.where(jnp.arange(D_IN) < D_A, HEAD_DIM ** -0.5, 1.0).astype(F32)
    return dict(
        g1=norm1_g[:, None, :],
        w_in=(w_in * q_scale).astype(BF16),
        wpool=_block_diag(pool_w).astype(BF16),
        pscale=pool_scale[:, None, :],
        convw=conv_w,
        convb=conv_b[:, None, :],
        wg=jnp.concatenate([_block_diag(gate_a_w), _block_diag(gate_x_w)], axis=2).astype(BF16),
        bg=jnp.concatenate([gate_a_b, gate_x_b], axis=1)[:, None, :],
        lam=lru_lambda[:, None, :],
        w_out=w_out.astype(BF16),
        g2=norm2_g[:, None, :],
        w_gu=w_gu.astype(BF16),
        w_down=w_down.astype(BF16),
    )


def kernel(x_prompt, x_sample, cache_win_k, cache_win_v, state_pool, state_conv, state_rglru, norm1_g, w_in, pool_w,
           pool_scale, conv_w, conv_b, gate_a_w, gate_a_b, gate_x_w, gate_x_b, lru_lambda, w_out, norm2_g, w_gu,
           w_down, final_g):
    batch, seq, _ = x_prompt.shape
    nb, t_new, _ = x_sample.shape
    depth, _, w_buf = cache_win_k.shape[:3]
    assert batch == 1 and seq % SUPER == 0 and seq % PROJ_TILE == 0 and seq % FFN_TILE == 0
    assert t_new == SUBLANES and w_buf == W_MAX and SUPER == W_MAX

    rope_p = _rope_tile_tables(seq)
    cos_s, sin_s = _rope_tables(PAST_LEN + jnp.arange(t_new, dtype=jnp.int32))
    cos_s = jnp.tile(cos_s, (nb, 1))
    sin_s = jnp.tile(sin_s, (nb, 1))
    counts = jnp.asarray(_sample_key_counts(w_buf, t_new, D_A_BLK // HEAD_DIM))
    fg = final_g[None, :]

    ck = jnp.transpose(cache_win_k, (0, 1, 3, 4, 2)).reshape(depth, nb, HEAD_SPLIT, D_A_BLK, w_buf)
    cv = jnp.transpose(cache_win_v, (0, 1, 3, 4, 2)).reshape(depth, nb, HEAD_SPLIT, D_A_BLK, w_buf)
    spool = jnp.pad(state_pool, ((0, 0), (0, 0), (1, 0), (0, 0)))
    sconv = jnp.pad(state_conv, ((0, 0), (0, 0), (SUBLANES - (CONV_W - 1), 0), (0, 0)))
    h0rows = jnp.repeat(state_rglru, t_new, axis=1)

    hp = x_prompt.reshape(seq, D_MODEL)
    hs = x_sample.reshape(nb * t_new, D_MODEL)
    pw = _stack_params(norm1_g, w_in, pool_w, pool_scale, conv_w, conv_b, gate_a_w, gate_a_b, gate_x_w, gate_x_b,
                       lru_lambda, w_out, norm2_g, w_gu, w_down)
    win_bufs = None
    p_k, p_v, p_pool, p_conv, p_h, s_pool, s_conv, s_h = [], [], [], [], [], [], [], []
    for l in range(depth):
        last = l == depth - 1
        qs, kn, vn, obc_s, npool, nconv, nh = _sample_proj(hs, l, pw, cos_s, sin_s, spool[l], sconv[l], h0rows[l])
        q, k, v, obc, pst, cst, hst = _prompt_proj(hp, l, pw, rope_p, PROJ_TILE)
        oa, kt, vt = _prompt_attn(q, k, v)
        hp, win_bufs, oa_s = _outffn_sample_attn(hp, oa, obc, l, pw, fg, last, FFN_TILE, ck, cv, qs, kn, vn,
                                                 win_bufs, counts)
        p_k.append(kt)
        p_v.append(vt)
        p_pool.append(pst[None, 1:, :])
        p_conv.append(cst[None, SUBLANES - (CONV_W - 1):, :])
        p_h.append(hst[SUBLANES - 1:, :])
        hs = _outffn(hs, oa_s, obc_s, l, pw, fg, last, nb * t_new)
        s_pool.append(npool[:, 1:, :])
        s_conv.append(nconv[:, SUBLANES - (CONV_W - 1):, :])
        s_h.append(nh.reshape(nb, t_new, D_C)[:, t_new - 1, :])

    y_prompt = hp.reshape(batch, seq, D_MODEL)
    y_sample = hs.reshape(nb, t_new, D_MODEL)
    s_win_k = jnp.transpose(win_bufs[0].reshape(depth, nb, N_HEADS_A, HEAD_DIM, w_buf), (0, 1, 4, 2, 3))
    s_win_v = jnp.transpose(win_bufs[1].reshape(depth, nb, N_HEADS_A, HEAD_DIM, w_buf), (0, 1, 4, 2, 3))
    p_win_k = jnp.transpose(jnp.stack(p_k).reshape(depth, 1, N_HEADS_A, HEAD_DIM, W_MAX), (0, 1, 4, 2, 3))
    p_win_v = jnp.transpose(jnp.stack(p_v).reshape(depth, 1, N_HEADS_A, HEAD_DIM, W_MAX), (0, 1, 4, 2, 3))
    return (y_prompt, y_sample, p_win_k, p_win_v, jnp.stack(p_pool), jnp.stack(p_conv),
            jnp.stack(p_h), s_win_k, s_win_v, jnp.stack(s_pool), jnp.stack(s_conv), jnp.stack(s_h))
```

```python
import functools

import numpy as np
import jax
import jax.numpy as jnp
from jax import lax
from jax.experimental import pallas as pl
from jax.experimental.pallas import tpu as pltpu

F32 = jnp.float32
BF16 = jnp.bfloat16

D_MODEL = 1024
HEAD_DIM = 64
D_A = 384
N_HEADS_A = D_A // HEAD_DIM
D_B = 256
D_C = 384
D_IN = 3 * D_A + D_B + 2 * D_C
D_FF = 2816
POOL_MAX = 16
CONV_W = 4
RG_C = 8.0
EPS = 1e-6
ROPE_THETA = 10000.0
W_MAX = 2048
PAST_LEN = 16384
DILATED_PATTERNS = ((128, 1), (512, 4), (2048, 16))
BAND = 128
MAX_DIL = 16
SUPER = BAND * MAX_DIL

LANES = 128
SUBLANES = 8
NEG = -1e30
VMEM_LIMIT = 56 * 1024 * 1024

PROJ_TILE = 1024
PROJ_SUB = 256
PROJ_AHEAD = 2
FFN_TILE = 512
HEAD_SPLIT = 3
D_A_BLK = D_A // HEAD_SPLIT

_Q0, _K0, _V0, _U0, _X0, _G0 = 0, D_A, 2 * D_A, 3 * D_A, 3 * D_A + D_B, 3 * D_A + D_B + D_C


def _rms(x, g):
    return x * lax.rsqrt(jnp.mean(x * x, axis=-1, keepdims=True) + EPS) * g


def _rope_tile(t, cos, sin_signed):
    lane = lax.broadcasted_iota(jnp.int32, t.shape, 1)
    first = (lane % HEAD_DIM) < (HEAD_DIM // 2)
    swapped = jnp.where(first, pltpu.roll(t, LANES - HEAD_DIM // 2, axis=1), pltpu.roll(t, HEAD_DIM // 2, axis=1))
    return t * cos + swapped * sin_signed


def _log_sigmoid(x):
    return jnp.minimum(x, 0.0) - jnp.log1p(jnp.exp(-jnp.abs(x)))


def _project(x, g1, w_in):
    hn = _rms(x, g1)
    return jnp.dot(hn.astype(BF16), w_in, preferred_element_type=F32)


def _pool_map(sums0, sums1, u, cnt0, cnt1, wpool, pscale):
    lo = lax.broadcasted_iota(jnp.int32, cnt0.shape, 1) < HEAD_DIM
    mean0 = jnp.where(lo, sums0[0], sums0[1]) / cnt0
    mean1 = jnp.where(lo, sums1[0], sums1[1]) / cnt1
    pooled = jnp.concatenate([mean0, mean1], axis=1) - u
    return jnp.dot(pooled.astype(BF16), wpool, preferred_element_type=F32) * pscale


def _window_sums_shifted(shift, narrow, wide):
    s = shift(0)
    for i in range(1, narrow):
        s = s + shift(i)
    s_narrow = s
    for i in range(narrow, wide):
        s = s + shift(i)
    return s_narrow, s


def _window_sums_rolled(xe, halo):
    s2 = xe + pltpu.roll(xe, 1, axis=0)
    s4 = s2 + pltpu.roll(s2, 2, axis=0)
    s8 = s4 + pltpu.roll(s4, 4, axis=0)
    s16 = s8 + pltpu.roll(s8, 8, axis=0)
    return s2[halo:], s4[halo:], s8[halo:], s16[halo:]


def _rglru_terms(xshift, convw, convb, wg, bg, lam):
    xc = convb + (xshift(3) * convw[0:1] + xshift(2) * convw[1:2] + xshift(1) * convw[2:3] + xshift(0) * convw[3:4])
    g = jnp.dot(xc.astype(BF16), wg, preferred_element_type=F32) + bg
    r = jax.nn.sigmoid(g[:, :D_C])
    ig = jax.nn.sigmoid(g[:, D_C:])
    half_log_a = (0.5 * RG_C) * r * _log_sigmoid(lam)
    a = jnp.exp(2.0 * half_log_a)
    z = -jnp.tanh(half_log_a)
    root = jnp.where(z > 0.0, z * lax.rsqrt(z), 0.0)
    b = (1.0 + a) * root * ig * xc
    return a, b


def _compose_step(a, b, shift):
    b = a * pltpu.roll(b, shift, axis=0) + b
    a = a * pltpu.roll(a, shift, axis=0)
    return a, b


def _scan_groups(a, b, h0_rows):
    row = lax.broadcasted_iota(jnp.int32, a.shape, 0) % SUBLANES
    for s in (1, 2, 4):
        keep = row >= s
        a_sh = jnp.where(keep, pltpu.roll(a, s, axis=0), 1.0)
        b_sh = jnp.where(keep, pltpu.roll(b, s, axis=0), 0.0)
        b = a * b_sh + b
        a = a * a_sh
    return a * h0_rows + b


def _scan_tile(a, b, h_last):
    n, c = a.shape
    a = jnp.concatenate([jnp.ones((SUBLANES, c), F32), a], axis=0)
    b = jnp.concatenate([jnp.zeros((SUBLANES, c), F32), b], axis=0)
    for s in (1, 2, 4):
        a, b = _compose_step(a, b, s)
    h = jnp.broadcast_to(h_last, (SUBLANES, c))
    out = []
    for j in range(1, n // SUBLANES + 1):
        rows = slice(j * SUBLANES, (j + 1) * SUBLANES)
        h = a[rows] * h + b[rows]
        out.append(h)
    return jnp.concatenate(out, axis=0)


def _prompt_proj_kernel(x_ref, g1_ref, win_ref, qs_ref, cr_ref, sr_ref, crs_ref, srs_ref, cb_ref, sb_ref,
                        wpool_ref, pscale_ref, convw_ref, convb_ref, wg_ref, bg_ref, lam_ref,
                        q_ref, k_ref, v_ref, obc_ref, pst_ref, cst_ref, hst_ref,
                        ubuf, xbuf, hcar, wbf):
    i = pl.program_id(0)
    tm = x_ref.shape[0]

    @pl.when(i == 0)
    def _():
        wbf[...] = (win_ref[...] * qs_ref[...]).astype(BF16)
        ubuf[0:POOL_MAX, :] = jnp.zeros((POOL_MAX, D_B), F32)
        xbuf[0:SUBLANES, :] = jnp.zeros((SUBLANES, D_C), F32)
        hcar[...] = jnp.zeros((SUBLANES, D_C), F32)

    def project(r0):
        rows = slice(r0, r0 + PROJ_SUB)
        proj = _project(x_ref[rows, :], g1_ref[...], wbf[...])
        sub = i * (tm // PROJ_SUB) + r0 // PROJ_SUB
        cb = cb_ref[pl.ds(sub, 1), :]
        sb = sb_ref[pl.ds(sub, 1), :]
        cos = cb * cr_ref[...] - sb * sr_ref[...]
        sin = sb * crs_ref[...] + cb * srs_ref[...]
        for hp in range(D_A // LANES):
            sl = slice(hp * LANES, (hp + 1) * LANES)
            q_ref[hp, rows, :] = _rope_tile(proj[:, _Q0:_K0][:, sl], cos, sin)
            k_ref[hp, rows, :] = _rope_tile(proj[:, _K0:_V0][:, sl], cos, sin)
            v_ref[hp, rows, :] = proj[:, _V0:_U0][:, sl]
        u = proj[:, _U0:_X0]
        xr = proj[:, _X0:_G0]
        ubuf[POOL_MAX + r0:POOL_MAX + r0 + PROJ_SUB, :] = u
        xbuf[SUBLANES + r0:SUBLANES + r0 + PROJ_SUB, :] = xr
        return u, xr, proj[:, _G0:]

    def mix(r0, u, gt, h_last):
        rows = slice(r0, r0 + PROJ_SUB)
        pos1 = (i * tm + r0 + 1 + lax.broadcasted_iota(jnp.int32, (PROJ_SUB, LANES), 0)).astype(F32)
        lo = lax.broadcasted_iota(jnp.int32, (PROJ_SUB, LANES), 1) < HEAD_DIM
        cnt0 = jnp.minimum(pos1, jnp.where(lo, 2.0, 4.0))
        cnt1 = jnp.minimum(pos1, jnp.where(lo, 8.0, 16.0))
        ext = slice(r0, r0 + PROJ_SUB + POOL_MAX)
        s2, s4, _, _ = _window_sums_rolled(ubuf[ext, 0:LANES], POOL_MAX)
        _, _, s8, s16 = _window_sums_rolled(ubuf[ext, LANES:D_B], POOL_MAX)
        o_b = _pool_map((s2, s4), (s8, s16), u, cnt0, cnt1, wpool_ref[...], pscale_ref[...])

        a, b = _rglru_terms(lambda s: xbuf[pl.ds(SUBLANES + r0 - s, PROJ_SUB), :], convw_ref[...],
                            convb_ref[...], wg_ref[...], bg_ref[...], lam_ref[...])
        h = _scan_tile(a, b, h_last)
        obc_ref[rows, 0:D_B] = o_b
        obc_ref[rows, D_B:] = h * jax.nn.gelu(gt)
        return h

    h_last = hcar[SUBLANES - 1:SUBLANES, :]
    starts = list(range(0, tm, PROJ_SUB))
    pending = [project(r0) for r0 in starts[:PROJ_AHEAD]]
    for k, r0 in enumerate(starts):
        u, xr, gt = pending.pop(0)
        if k + PROJ_AHEAD < len(starts):
            pending.append(project(starts[k + PROJ_AHEAD]))
        h = mix(r0, u, gt, h_last)
        h_last = h[PROJ_SUB - 1:, :]

    ubuf[0:POOL_MAX, :] = u[PROJ_SUB - POOL_MAX:, :]
    xbuf[0:SUBLANES, :] = xr[PROJ_SUB - SUBLANES:, :]
    hcar[...] = h[PROJ_SUB - SUBLANES:, :]
    pst_ref[...] = u[PROJ_SUB - POOL_MAX:, :]
    cst_ref[...] = xr[PROJ_SUB - SUBLANES:, :]
    hst_ref[...] = h[PROJ_SUB - SUBLANES:, :]


def _const_spec(shape):
    return pl.BlockSpec(shape, lambda *_: (0,) * len(shape))


def _layer_spec(layer, shape):
    return pl.BlockSpec((None,) + tuple(shape), lambda *_: (layer,) + (0,) * len(shape),
                        pipeline_mode=pl.Buffered(1))


def _prompt_proj(x, layer, pw, rope, tm):
    t = x.shape[0]
    n_hp = D_A // LANES
    qkv_shape = jax.ShapeDtypeStruct((n_hp, t, LANES), F32)
    qkv_spec = pl.BlockSpec((n_hp, tm, LANES), lambda i: (0, i, 0))
    return pl.pallas_call(
        _prompt_proj_kernel,
        grid=(t // tm,),
        in_specs=([pl.BlockSpec((tm, D_MODEL), lambda i: (i, 0))]
                  + [_layer_spec(layer, pw[k].shape[1:]) for k in _PROJ_PARAMS]
                  + [_const_spec(r.shape) for r in rope]
                  + [_layer_spec(layer, pw[k].shape[1:]) for k in _MIX_PARAMS]),
        out_specs=[
            qkv_spec, qkv_spec, qkv_spec,
            pl.BlockSpec((tm, D_B + D_C), lambda i: (i, 0)),
            _const_spec((POOL_MAX, D_B)),
            _const_spec((SUBLANES, D_C)),
            _const_spec((SUBLANES, D_C)),
        ],
        out_shape=[
            qkv_shape, qkv_shape, qkv_shape,
            jax.ShapeDtypeStruct((t, D_B + D_C), F32),
            jax.ShapeDtypeStruct((POOL_MAX, D_B), F32),
            jax.ShapeDtypeStruct((SUBLANES, D_C), F32),
            jax.ShapeDtypeStruct((SUBLANES, D_C), F32),
        ],
        scratch_shapes=[
            pltpu.VMEM((POOL_MAX + tm, D_B), F32),
            pltpu.VMEM((SUBLANES + tm, D_C), F32),
            pltpu.VMEM((SUBLANES, D_C), F32),
            pltpu.VMEM((D_MODEL, D_IN), BF16),
        ],
        compiler_params=pltpu.CompilerParams(dimension_semantics=("arbitrary",), vmem_limit_bytes=VMEM_LIMIT),
        name="prompt_proj",
    )(x, *[pw[k] for k in _PROJ_PARAMS], *rope, *[pw[k] for k in _MIX_PARAMS])


def _band_bias(dil):
    slabs = MAX_DIL // dil
    rows = BAND // slabs
    row = lax.broadcasted_iota(jnp.int32, (BAND, 2 * BAND), 0)
    col = lax.broadcasted_iota(jnp.int32, (BAND, 2 * BAND), 1)
    qi = slabs * (row % rows) + row // rows
    ki = slabs * (col % (2 * rows)) + col // (2 * rows)
    valid = (ki >= qi) & (ki <= qi + BAND)
    bias = jnp.where(valid, 0.0, NEG)
    bias_first = jnp.where(valid & (ki >= BAND), 0.0, NEG)
    return bias, bias_first


def _prompt_attn_kernel(q_ref, k_ref, v_ref, o_ref, kt_ref, vt_ref, q16, kcat, vcat, acc_s, m_s, l_s):
    i = pl.program_id(1)

    @pl.when(i == pl.num_programs(1) - 1)
    def _():
        kt_ref[...] = k_ref[...].T
        vt_ref[...] = v_ref[...].T

    @pl.when(i == 0)
    def _():
        kcat[0:BAND, :] = jnp.zeros((BAND, MAX_DIL * LANES), F32)
        vcat[0:BAND, :] = jnp.zeros((BAND, MAX_DIL * LANES), F32)

    @pl.when(i > 0)
    def _():
        kcat[0:BAND, :] = kcat[BAND:, :]
        vcat[0:BAND, :] = vcat[BAND:, :]

    for r in range(MAX_DIL):
        ls = slice(r * LANES, (r + 1) * LANES)
        q16[:, ls] = q_ref[pl.ds(r, BAND, stride=MAX_DIL), :]
        kcat[BAND:, ls] = k_ref[pl.ds(r, BAND, stride=MAX_DIL), :]
        vcat[BAND:, ls] = v_ref[pl.ds(r, BAND, stride=MAX_DIL), :]

    head0 = lax.broadcasted_iota(jnp.int32, (BAND, LANES), 1) < HEAD_DIM
    ones_cols = jnp.ones((2 * BAND, LANES), BF16)

    units = []
    for stage, (_, dil) in enumerate(DILATED_PATTERNS):
        slabs = MAX_DIL // dil
        rows = BAND // slabs
        bias, bias_first = _band_bias(dil)
        bias_blk0 = jnp.where(i == 0, bias_first, bias)
        for blk in range(slabs):
            for res in range(dil):
                units.append((stage, slabs, rows, blk, [res + dil * c for c in range(slabs)],
                              bias_blk0 if blk == 0 else bias))

    def scores(unit):
        _, _, rows, blk, groups, _ = unit
        q_rows = slice(rows * blk, rows * (blk + 1))
        k_rows = slice(BAND + rows * (blk - 1), BAND + rows * (blk + 1))
        lanes = [slice(g * LANES, (g + 1) * LANES) for g in groups]
        qb = jnp.concatenate([q16[q_rows, ls] for ls in lanes], axis=0)
        kk = jnp.concatenate([kcat[k_rows, ls] for ls in lanes], axis=0).astype(BF16)
        vv = jnp.concatenate([vcat[k_rows, ls] for ls in lanes], axis=0).astype(BF16)
        q2 = jnp.concatenate([jnp.where(head0, qb, 0.0), jnp.where(head0, 0.0, qb)], axis=0).astype(BF16)
        s2 = lax.dot_general(q2, kk, (((1,), (1,)), ((), ())), preferred_element_type=F32)
        return s2, jnp.concatenate([vv, ones_cols], axis=1)

    def finish(unit, s2, vv1):
        stage, _, rows, blk, groups, b_u = unit
        q_rows = slice(rows * blk, rows * (blk + 1))
        lanes = [slice(g * LANES, (g + 1) * LANES) for g in groups]
        ms, ps = [], []
        for hh in range(2):
            s = s2[hh * BAND:(hh + 1) * BAND] + b_u
            m = jnp.max(s, axis=1, keepdims=True)
            ms.append(m)
            ps.append(jnp.exp((s - m).astype(BF16)))
        pv2 = jnp.dot(jnp.concatenate(ps, axis=0), vv1, preferred_element_type=F32)
        parts = [(ms[hh], pv2[hh * BAND:(hh + 1) * BAND, LANES:], pv2[hh * BAND:(hh + 1) * BAND, :LANES])
                 for hh in range(2)]
        m_b = jnp.where(head0, parts[0][0], parts[1][0])
        l_b = jnp.where(head0, parts[0][1], parts[1][1])
        pv_b = jnp.where(head0, parts[0][2], parts[1][2])
        if stage > 0:
            m_o = jnp.concatenate([m_s[q_rows, ls] for ls in lanes], axis=0)
            l_o = jnp.concatenate([l_s[q_rows, ls] for ls in lanes], axis=0)
            a_o = jnp.concatenate([acc_s[q_rows, ls] for ls in lanes], axis=0)
            m_n = jnp.maximum(m_o, m_b)
            w_o = jnp.exp(m_o - m_n)
            w_b = jnp.exp(m_b - m_n)
            l_b = l_o * w_o + l_b * w_b
            pv_b = a_o * w_o + pv_b * w_b
            m_b = m_n
        last = stage == len(DILATED_PATTERNS) - 1
        if last:
            out = pv_b / l_b
        for c, ls in enumerate(lanes):
            sub = slice(c * rows, (c + 1) * rows)
            if last:
                o_ref[pl.ds(MAX_DIL * rows * blk + groups[c], rows, stride=MAX_DIL), :] = out[sub]
            else:
                m_s[q_rows, ls] = m_b[sub]
                l_s[q_rows, ls] = l_b[sub]
                acc_s[q_rows, ls] = pv_b[sub]

    for unit in units:
        finish(unit, *scores(unit))


def _prompt_attn(q, k, v):
    n_hp, t, _ = q.shape
    n_super = t // SUPER
    width = MAX_DIL * LANES
    cur = pl.BlockSpec((None, SUPER, LANES), lambda h, i: (h, i, 0))
    win = pl.BlockSpec((None, LANES, SUPER), lambda h, i: (h, 0, 0))
    win_shape = jax.ShapeDtypeStruct((n_hp, LANES, SUPER), F32)
    return pl.pallas_call(
        _prompt_attn_kernel,
        grid=(n_hp, n_super),
        in_specs=[cur, cur, cur],
        out_specs=[cur, win, win],
        out_shape=[jax.ShapeDtypeStruct(q.shape, F32), win_shape, win_shape],
        scratch_shapes=[
            pltpu.VMEM((BAND, width), F32),
            pltpu.VMEM((2 * BAND, width), F32),
            pltpu.VMEM((2 * BAND, width), F32),
            pltpu.VMEM((BAND, width), F32),
            pltpu.VMEM((BAND, width), F32),
            pltpu.VMEM((BAND, width), F32),
        ],
        compiler_params=pltpu.CompilerParams(dimension_semantics=("arbitrary", "arbitrary"),
                                             vmem_limit_bytes=VMEM_LIMIT),
        name="prompt_attn",
    )(q, k, v)


FF_CHUNK = 256


def _outffn_begin(x_ref, oa_ref, obc_ref, wout_ref, g2_ref):
    mix = jnp.concatenate([oa_ref[c] for c in range(oa_ref.shape[0])] + [obc_ref[...]], axis=1).astype(BF16)
    x1 = x_ref[...] + jnp.dot(mix, wout_ref[...], preferred_element_type=F32)
    return x1, _rms(x1, g2_ref[...]).astype(BF16)


def _ffn_chunks(h2, acc, wgu_ref, wdown_ref, chunks):
    for c in chunks:
        g = jnp.dot(h2, wgu_ref[:, c * FF_CHUNK:(c + 1) * FF_CHUNK], preferred_element_type=F32)
        up = jnp.dot(h2, wgu_ref[:, D_FF + c * FF_CHUNK:D_FF + (c + 1) * FF_CHUNK], preferred_element_type=F32)
        act = (jax.nn.silu(g) * up).astype(BF16)
        acc = acc + jnp.dot(act, wdown_ref[c * FF_CHUNK:(c + 1) * FF_CHUNK, :], preferred_element_type=F32)
    return acc


def _outffn_rows(x_ref, oa_ref, obc_ref, wout_ref, g2_ref, wgu_ref, wdown_ref, fg_ref, final):
    x1, h2 = _outffn_begin(x_ref, oa_ref, obc_ref, wout_ref, g2_ref)
    acc = _ffn_chunks(h2, x1, wgu_ref, wdown_ref, range(D_FF // FF_CHUNK))
    return _rms(acc, fg_ref[...]) if final else acc


def _outffn_kernel(x_ref, oa_ref, obc_ref, wout_ref, g2_ref, wgu_ref, wdown_ref, fg_ref, o_ref, *, final):
    o_ref[...] = _outffn_rows(x_ref, oa_ref, obc_ref, wout_ref, g2_ref, wgu_ref, wdown_ref, fg_ref, final)


def _outffn_specs(oa, layer, pw, tm, index):
    return [
        pl.BlockSpec((tm, D_MODEL), lambda *g: (index(*g), 0)),
        pl.BlockSpec((oa.shape[0], tm, oa.shape[2]), lambda *g: (0, index(*g), 0)),
        pl.BlockSpec((tm, D_B + D_C), lambda *g: (index(*g), 0)),
    ] + [_layer_spec(layer, pw[k].shape[1:]) for k in _FFN_PARAMS] + [_const_spec((1, D_MODEL))]


def _outffn(x, oa, obc, layer, pw, final_g, final, tm):
    t = x.shape[0]
    return pl.pallas_call(
        functools.partial(_outffn_kernel, final=final),
        grid=(t // tm,),
        in_specs=_outffn_specs(oa, layer, pw, tm, lambda i: i),
        out_specs=pl.BlockSpec((tm, D_MODEL), lambda i: (i, 0)),
        out_shape=jax.ShapeDtypeStruct((t, D_MODEL), F32),
        compiler_params=pltpu.CompilerParams(dimension_semantics=("parallel",), vmem_limit_bytes=VMEM_LIMIT),
        name="outffn",
    )(x, oa, obc, *[pw[k] for k in _FFN_PARAMS], final_g)


def _sample_proj_kernel(x_ref, g1_ref, win_ref, qs_ref, cos_ref, sin_ref, wpool_ref, pscale_ref, convw_ref, convb_ref,
                        wg_ref, bg_ref, lam_ref, spool_ref, sconv_ref, h0_ref,
                        q_ref, k_ref, v_ref, obc_ref, npool_ref, nconv_ref, nh_ref,
                        ubuf, xbuf):
    n = x_ref.shape[0]
    nb = n // SUBLANES
    proj = _project(x_ref[...], g1_ref[...], (win_ref[...] * qs_ref[...]).astype(BF16))
    cos = cos_ref[...]
    sin = sin_ref[...]
    for c in range(HEAD_SPLIT):
        sl = slice(c * D_A_BLK, (c + 1) * D_A_BLK)
        q_ref[c] = _rope_tile(proj[:, _Q0:_K0][:, sl], cos, sin)
        k_ref[c] = _rope_tile(proj[:, _K0:_V0][:, sl], cos, sin)
        v_ref[c] = proj[:, _V0:_U0][:, sl]

    u = proj[:, _U0:_X0]
    xr = proj[:, _X0:_G0]
    gt = proj[:, _G0:]
    ubuf[:, 0:POOL_MAX, :] = spool_ref[...]
    ubuf[:, POOL_MAX:, :] = u.reshape(nb, SUBLANES, D_B)
    xbuf[:, 0:SUBLANES, :] = sconv_ref[...]
    xbuf[:, SUBLANES:, :] = xr.reshape(nb, SUBLANES, D_C)

    lo = lax.broadcasted_iota(jnp.int32, (n, LANES), 1) < HEAD_DIM
    cnt0 = jnp.where(lo, 2.0, 4.0)
    cnt1 = jnp.where(lo, 8.0, 16.0)
    sums0 = _window_sums_shifted(lambda s: ubuf[:, pl.ds(POOL_MAX - s, SUBLANES), 0:LANES].reshape(n, LANES), 2, 4)
    sums1 = _window_sums_shifted(lambda s: ubuf[:, pl.ds(POOL_MAX - s, SUBLANES), LANES:D_B].reshape(n, LANES), 8, 16)
    o_b = _pool_map(sums0, sums1, u, cnt0, cnt1, wpool_ref[...], pscale_ref[...])

    a, b = _rglru_terms(lambda s: xbuf[:, pl.ds(SUBLANES - s, SUBLANES), :].reshape(n, D_C), convw_ref[...],
                        convb_ref[...], wg_ref[...], bg_ref[...], lam_ref[...])
    h = _scan_groups(a, b, h0_ref[...])
    obc_ref[:, 0:D_B] = o_b
    obc_ref[:, D_B:] = h * jax.nn.gelu(gt)
    npool_ref[...] = ubuf[:, SUBLANES:, :]
    nconv_ref[...] = xr.reshape(nb, SUBLANES, D_C)
    nh_ref[...] = h


def _sample_proj(x, layer, pw, cos, sin, spool, sconv, h0rows):
    n = x.shape[0]
    nb = n // SUBLANES
    shapes = [
        jax.ShapeDtypeStruct((HEAD_SPLIT, n, D_A_BLK), F32), jax.ShapeDtypeStruct((HEAD_SPLIT, n, D_A_BLK), F32),
        jax.ShapeDtypeStruct((HEAD_SPLIT, n, D_A_BLK), F32),
        jax.ShapeDtypeStruct((n, D_B + D_C), F32),
        jax.ShapeDtypeStruct((nb, POOL_MAX, D_B), F32),
        jax.ShapeDtypeStruct((nb, SUBLANES, D_C), F32),
        jax.ShapeDtypeStruct((n, D_C), F32),
    ]
    data = (spool, sconv, h0rows)
    in_specs = ([_const_spec(x.shape)] + [_layer_spec(layer, pw[k].shape[1:]) for k in _PROJ_PARAMS]
                + [_const_spec(cos.shape), _const_spec(sin.shape)]
                + [_layer_spec(layer, pw[k].shape[1:]) for k in _MIX_PARAMS] + [_const_spec(a.shape) for a in data])
    args = (x, *[pw[k] for k in _PROJ_PARAMS], cos, sin, *[pw[k] for k in _MIX_PARAMS], *data)
    return pl.pallas_call(
        _sample_proj_kernel,
        grid=(1,),
        in_specs=in_specs,
        out_specs=[_const_spec(s.shape) for s in shapes],
        out_shape=shapes,
        scratch_shapes=[
            pltpu.VMEM((nb, POOL_MAX + SUBLANES, D_B), F32),
            pltpu.VMEM((nb, 2 * SUBLANES, D_C), F32),
        ],
        compiler_params=pltpu.CompilerParams(dimension_semantics=("arbitrary",), vmem_limit_bytes=VMEM_LIMIT),
        name="sample_proj",
    )(*args)


def _sample_key_counts(w_buf, t_new, n_heads):
    k = np.arange(w_buf + LANES)[None, :]
    dist = w_buf + np.arange(t_new)[:, None] - k
    cnt = np.zeros(dist.shape, np.float32)
    for window, dil in DILATED_PATTERNS:
        cnt += ((dist >= 0) & (dist <= window) & (dist % dil == 0)).astype(np.float32)
    return np.tile(cnt, (n_heads, 1))


def _shift_window(c_ref, new_ref, s_ref):
    da, w_buf = c_ref.shape
    t_new = new_ref.shape[0]
    tail = lax.broadcasted_iota(jnp.int32, (da, LANES), 1) >= LANES - t_new
    rolled = pltpu.roll(c_ref[...], w_buf - t_new, axis=1)
    new_t = jnp.concatenate([jnp.zeros((LANES - t_new, da), F32), new_ref[...]], axis=0).T
    s_ref[:, 0:w_buf - LANES] = rolled[:, 0:w_buf - LANES]
    s_ref[:, w_buf - LANES:] = jnp.where(tail, new_t, rolled[:, w_buf - LANES:])


def _sample_attn_scores(ck_ref, q_ref, kn_ref):
    da = ck_ref.shape[0]
    t_new = q_ref.shape[0]
    qt = jnp.concatenate([q_ref[...]] * (da // HEAD_DIM), axis=0)
    row = lax.broadcasted_iota(jnp.int32, qt.shape, 0)
    lane = lax.broadcasted_iota(jnp.int32, qt.shape, 1)
    own = (row // t_new) == (lane // HEAD_DIM)
    qbd = jnp.where(own, qt, 0.0).astype(BF16)
    kn_pad = jnp.concatenate([kn_ref[...], jnp.zeros((LANES - t_new, da), F32)], axis=0).astype(BF16)
    return jnp.concatenate([
        jnp.dot(qbd, ck_ref[...].astype(BF16), preferred_element_type=F32),
        lax.dot_general(qbd, kn_pad, (((1,), (1,)), ((), ())), preferred_element_type=F32)], axis=1)


def _sample_attn_finish(s, cnt_ref, cv_ref, vn_ref, oa_ref):
    da, w_buf = cv_ref.shape
    t_new = vn_ref.shape[0]
    cv = cv_ref[...]
    vn = vn_ref[...]
    zpad = jnp.zeros((LANES - t_new, da), F32)

    cnt = cnt_ref[...]
    s = jnp.where(cnt > 0.0, s, NEG)
    m = jnp.max(s, axis=1, keepdims=True)
    p = jnp.exp(s - m) * cnt
    den = jnp.sum(p, axis=1, keepdims=True)
    pb = p.astype(BF16)
    vn_pad = jnp.concatenate([vn, zpad], axis=0).astype(BF16)
    o_t = lax.dot_general(cv.astype(BF16), pb[:, :w_buf], (((1,), (1,)), ((), ())), preferred_element_type=F32)
    o_c = jnp.concatenate([o_t, jnp.zeros((da, LANES - o_t.shape[1]), F32)], axis=1).T[:o_t.shape[1]]
    o_n = (o_c + jnp.dot(pb[:, w_buf:], vn_pad, preferred_element_type=F32)) / den
    out = jnp.zeros((t_new, da), F32)
    lane_o = lax.broadcasted_iota(jnp.int32, (t_new, da), 1) // HEAD_DIM
    for h in range(da // HEAD_DIM):
        out = jnp.where(lane_o == h, o_n[h * t_new:(h + 1) * t_new, :], out)
    oa_ref[...] = out


def _outffn_sample_attn_kernel(x_ref, oa_ref, obc_ref, wout_ref, g2_ref, wgu_ref, wdown_ref, fg_ref,
                               cnt_ref, ck_ref, cv_ref, q_ref, kn_ref, vn_ref, *rest, final, parts):
    o_ref, sk_ref, sv_ref, oas_ref, acc_scr, h2_scr = rest[-6:]

    for part, chunks in enumerate(parts):
        @pl.when(pl.program_id(1) == part)
        def _(part=part, chunks=chunks):
            scores = _sample_attn_scores(ck_ref, q_ref, kn_ref)
            _shift_window(ck_ref, kn_ref, sk_ref)
            _shift_window(cv_ref, vn_ref, sv_ref)
            if part == 0:
                acc, h2 = _outffn_begin(x_ref, oa_ref, obc_ref, wout_ref, g2_ref)
                h2_scr[...] = h2
            else:
                acc, h2 = acc_scr[...], h2_scr[...]
            acc = _ffn_chunks(h2, acc, wgu_ref, wdown_ref, chunks)
            if part == len(parts) - 1:
                o_ref[...] = _rms(acc, fg_ref[...]) if final else acc
            else:
                acc_scr[...] = acc
            _sample_attn_finish(scores, cnt_ref, cv_ref, vn_ref, oas_ref)


def _outffn_sample_attn(x, oa, obc, layer, pw, final_g, final, tm, cache_k, cache_v, q, kn, vn, win_bufs, counts):
    t = x.shape[0]
    n_tiles = t // tm
    depth, nb, split, da, w_buf = cache_k.shape
    t_new = q.shape[1] // nb
    per_tile = nb * split // n_tiles
    assert per_tile * n_tiles == nb * split
    parts = [tuple(int(c) for c in p) for p in np.array_split(np.arange(D_FF // FF_CHUNK), per_tile)[::-1]]

    def group(i, j):
        g = i * per_tile + j
        return g // split, g % split

    def cache_index(i, j):
        b, c = group(i, j)
        return (layer, b, c, 0, 0)

    def tok_index(i, j):
        b, c = group(i, j)
        return (c, b, 0)

    cache_spec = pl.BlockSpec((None, None, None, da, w_buf), cache_index)
    tok_spec = pl.BlockSpec((None, t_new, da), tok_index)
    in_specs = _outffn_specs(oa, layer, pw, tm, lambda i, j: i) + [
        _const_spec(counts.shape), cache_spec, cache_spec, tok_spec, tok_spec, tok_spec]
    args = [x, oa, obc, *[pw[k] for k in _FFN_PARAMS], final_g, counts, cache_k, cache_v, q, kn, vn]
    aliases = {}
    if win_bufs is not None:
        in_specs += [pl.BlockSpec(memory_space=pl.ANY)] * 2
        aliases = {len(args): 1, len(args) + 1: 2}
        args += list(win_bufs)
    win_shape = jax.ShapeDtypeStruct(cache_k.shape, F32)
    y, sk, sv, oa_s = pl.pallas_call(
        functools.partial(_outffn_sample_attn_kernel, final=final, parts=parts),
        grid=(n_tiles, per_tile),
        in_specs=in_specs,
        out_specs=[pl.BlockSpec((tm, D_MODEL), lambda i, j: (i, 0)), cache_spec, cache_spec, tok_spec],
        out_shape=[jax.ShapeDtypeStruct((t, D_MODEL), F32), win_shape, win_shape, jax.ShapeDtypeStruct(q.shape, F32)],
        input_output_aliases=aliases,
        scratch_shapes=[pltpu.VMEM((tm, D_MODEL), F32), pltpu.VMEM((tm, D_MODEL), BF16)],
        compiler_params=pltpu.CompilerParams(dimension_semantics=("arbitrary", "arbitrary"),
                                             vmem_limit_bytes=VMEM_LIMIT),
        name="outffn_sample_attn",
    )(*args)
    return y, (sk, sv), oa_s


def _rope_tables(pos):
    half = HEAD_DIM // 2
    inv = jnp.power(ROPE_THETA, -2.0 * jnp.arange(half, dtype=F32) / HEAD_DIM)
    ang = pos.astype(F32)[:, None] * inv[None, :]
    cos = jnp.tile(jnp.cos(ang), (1, LANES // half))
    sin = jnp.sin(ang)
    sin_signed = jnp.tile(jnp.concatenate([-sin, sin], axis=1), (1, LANES // HEAD_DIM))
    return cos, sin_signed


def _rope_tile_tables(seq):
    half = HEAD_DIM // 2
    inv = jnp.power(ROPE_THETA, -2.0 * jnp.arange(half, dtype=F32) / HEAD_DIM)
    inv = jnp.tile(inv, LANES // half)[None, :]
    sign = jnp.tile(jnp.concatenate([-jnp.ones(half, F32), jnp.ones(half, F32)]), LANES // HEAD_DIM)[None, :]
    ang_r = jnp.arange(PROJ_SUB, dtype=F32)[:, None] * inv
    ang_b = (jnp.arange(seq // PROJ_SUB, dtype=F32) * PROJ_SUB)[:, None] * inv
    cr, sr = jnp.cos(ang_r), jnp.sin(ang_r)
    return cr, sr, sign * cr, sign * sr, jnp.cos(ang_b), jnp.sin(ang_b)


def _block_diag(w):
    depth, n, c, d = w.shape
    eye = jnp.eye(n, dtype=w.dtype)
    return (eye[None, :, None, :, None] * w[:, :, :, None, :]).reshape(depth, n * c, n * d)


_PROJ_PARAMS = ("g1", "w_in", "qs")
_MIX_PARAMS = ("wpool", "pscale", "convw", "convb", "wg", "bg", "lam")
_FFN_PARAMS = ("w_out", "g2", "w_gu", "w_down")


def _stack_params(norm1_g, w_in, pool_w, pool_scale, conv_w, conv_b, gate_a_w, gate_a_b, gate_x_w, gate_x_b, lru_lambda,
                  w_out, norm2_g, w_gu, w_down):
    q_scale = jnp.where(jnp.arange(D_IN) < D_A, HEAD_DIM ** -0.5, 1.0).astype(F32)
    return dict(
        g1=norm1_g[:, None, :],
        w_in=w_in,
        qs=jnp.broadcast_to(q_scale, (w_in.shape[0], 1, D_IN)),
        wpool=_block_diag(pool_w).astype(BF16),
        pscale=pool_scale[:, None, :],
        convw=conv_w,
        convb=conv_b[:, None, :],
        wg=jnp.concatenate([_block_diag(gate_a_w), _block_diag(gate_x_w)], axis=2).astype(BF16),
        bg=jnp.concatenate([gate_a_b, gate_x_b], axis=1)[:, None, :],
        lam=lru_lambda[:, None, :],
        w_out=w_out.astype(BF16),
        g2=norm2_g[:, None, :],
        w_gu=w_gu.astype(BF16),
        w_down=w_down.astype(BF16),
    )


def kernel(x_prompt, x_sample, cache_win_k, cache_win_v, state_pool, state_conv, state_rglru, norm1_g, w_in, pool_w,
           pool_scale, conv_w, conv_b, gate_a_w, gate_a_b, gate_x_w, gate_x_b, lru_lambda, w_out, norm2_g, w_gu,
           w_down, final_g):
    batch, seq, _ = x_prompt.shape
    nb, t_new, _ = x_sample.shape
    depth, _, w_buf = cache_win_k.shape[:3]
    assert batch == 1 and seq % SUPER == 0 and seq % PROJ_TILE == 0 and seq % FFN_TILE == 0
    assert t_new == SUBLANES and w_buf == W_MAX and SUPER == W_MAX

    rope_p = _rope_tile_tables(seq)
    cos_s, sin_s = _rope_tables(PAST_LEN + jnp.arange(t_new, dtype=jnp.int32))
    cos_s = jnp.tile(cos_s, (nb, 1))
    sin_s = jnp.tile(sin_s, (nb, 1))
    counts = jnp.asarray(_sample_key_counts(w_buf, t_new, D_A_BLK // HEAD_DIM))
    fg = final_g[None, :]

    ck = jnp.transpose(cache_win_k, (0, 1, 3, 4, 2)).reshape(depth, nb, HEAD_SPLIT, D_A_BLK, w_buf)
    cv = jnp.transpose(cache_win_v, (0, 1, 3, 4, 2)).reshape(depth, nb, HEAD_SPLIT, D_A_BLK, w_buf)
    spool = jnp.pad(state_pool, ((0, 0), (0, 0), (1, 0), (0, 0)))
    sconv = jnp.pad(state_conv, ((0, 0), (0, 0), (SUBLANES - (CONV_W - 1), 0), (0, 0)))
    h0rows = jnp.repeat(state_rglru, t_new, axis=1)

    hp = x_prompt.reshape(seq, D_MODEL)
    hs = x_sample.reshape(nb * t_new, D_MODEL)
    pw = _stack_params(norm1_g, w_in, pool_w, pool_scale, conv_w, conv_b, gate_a_w, gate_a_b, gate_x_w, gate_x_b,
                       lru_lambda, w_out, norm2_g, w_gu, w_down)
    win_bufs = None
    p_k, p_v, p_pool, p_conv, p_h, s_pool, s_conv, s_h = [], [], [], [], [], [], [], []
    for l in range(depth):
        last = l == depth - 1
        qs, kn, vn, obc_s, npool, nconv, nh = _sample_proj(hs, l, pw, cos_s, sin_s, spool[l], sconv[l], h0rows[l])
        q, k, v, obc, pst, cst, hst = _prompt_proj(hp, l, pw, rope_p, PROJ_TILE)
        oa, kt, vt = _prompt_attn(q, k, v)
        hp, win_bufs, oa_s = _outffn_sample_attn(hp, oa, obc, l, pw, fg, last, FFN_TILE, ck, cv, qs, kn, vn,
                                                 win_bufs, counts)
        p_k.append(kt)
        p_v.append(vt)
        p_pool.append(pst[None, 1:, :])
        p_conv.append(cst[None, SUBLANES - (CONV_W - 1):, :])
        p_h.append(hst[SUBLANES - 1:, :])
        hs = _outffn(hs, oa_s, obc_s, l, pw, fg, last, nb * t_new)
        s_pool.append(npool[:, 1:, :])
        s_conv.append(nconv[:, SUBLANES - (CONV_W - 1):, :])
        s_h.append(nh.reshape(nb, t_new, D_C)[:, t_new - 1, :])

    y_prompt = hp.reshape(batch, seq, D_MODEL)
    y_sample = hs.reshape(nb, t_new, D_MODEL)
    s_win_k = jnp.transpose(win_bufs[0].reshape(depth, nb, N_HEADS_A, HEAD_DIM, w_buf), (0, 1, 4, 2, 3))
    s_win_v = jnp.transpose(win_bufs[1].reshape(depth, nb, N_HEADS_A, HEAD_DIM, w_buf), (0, 1, 4, 2, 3))
    p_win_k = jnp.transpose(jnp.stack(p_k).reshape(depth, 1, N_HEADS_A, HEAD_DIM, W_MAX), (0, 1, 4, 2, 3))
    p_win_v = jnp.transpose(jnp.stack(p_v).reshape(depth, 1, N_HEADS_A, HEAD_DIM, W_MAX), (0, 1, 4, 2, 3))
    return (y_prompt, y_sample, p_win_k, p_win_v, jnp.stack(p_pool), jnp.stack(p_conv),
            jnp.stack(p_h), s_win_k, s_win_v, jnp.stack(s_pool), jnp.stack(s_conv), jnp.stack(s_h))
```
